```python
import math
import jax, jax.numpy as jnp
from jax import lax
import numpy as np

D_MODEL = 2048
BATCH = 16
SEQ = 2048
DEPTH = 1
DEC_BATCH = 16
DEC_SEQ = 64
PAST_LEN = 2048

CHUNK = 64
N_META = 16
EPS = 1e-6
D_MIX = D_MODEL
A_HEAD_DIM = 64
A_WIDTH = D_MIX // 2
A_HEADS = A_WIDTH // A_HEAD_DIM
DECAY_RANK = 64
AAA_RANK = 64
GATE_RANK = 160
LNX_EPS = 64e-5
B_HEAD_DIM = 64
B_WIDTH = D_MIX - A_WIDTH
B_HEADS = B_WIDTH // B_HEAD_DIM
B_GROUPS = 2
D_STATE = 128
CONV_W = 4
B_CONV_DIM = B_WIDTH + 2 * B_GROUPS * D_STATE
A_COLS = 3 * A_WIDTH + DECAY_RANK + AAA_RANK + GATE_RANK
B_COLS = B_WIDTH + B_CONV_DIM + B_HEADS
IN_COLS = A_COLS + B_COLS
D_FF = ((8 * D_MODEL // 3 + 255) // 256) * 256

kernel_name = 'hybrid_rwkv7_mamba2_stream'

F32 = jnp.float32


def rmsnorm(x, w):
    xf = x.astype(F32)
    y = xf * lax.rsqrt(jnp.mean(xf * xf, axis=-1, keepdims=True) + EPS)
    return (y * w.astype(F32)).astype(x.dtype)


def wkv7_scan(r, w, k, v, kk, ka, S0):
    def step(S, inp):
        r_t, w_t, k_t, v_t, kk_t, ka_t = inp
        sa = jnp.einsum('bhij,bhj->bhi', S, kk_t)
        S = S * w_t[:, :, None, :] - sa[..., None] * ka_t[:, :, None, :] + v_t[..., None] * k_t[:, :, None, :]
        return S, jnp.einsum('bhij,bhj->bhi', S, r_t)
    xs = tuple(jnp.moveaxis(t, 1, 0) for t in (r, w, k, v, kk, ka))
    S, ys = lax.scan(step, S0, xs)
    return jnp.moveaxis(ys, 0, 1), S


def rwkv7_mix(za, shift0, wkv0, mu, w0, w_up, a0, a_up, g_up, k_k, k_a, r_k, lnx_w, lnx_b):
    Bsz, L, _ = za.shape
    z_prev = jnp.concatenate([shift0.astype(F32)[:, None, :], za[:, :-1]], axis=1)
    zm = za + mu * (z_prev - za)
    r, k, v, wd, ad, gd = jnp.split(zm, [A_WIDTH, 2 * A_WIDTH, 3 * A_WIDTH, 3 * A_WIDTH + DECAY_RANK,
                                         3 * A_WIDTH + DECAY_RANK + AAA_RANK], axis=-1)
    w = -jax.nn.softplus(-(w0 + jnp.tanh(wd) @ w_up)) - 0.5
    decay = jnp.exp(-jnp.exp(w))
    a = jax.nn.sigmoid(a0 + ad @ a_up)
    g = jax.nn.sigmoid(gd) @ g_up
    heads = lambda t: t.reshape(Bsz, L, A_HEADS, A_HEAD_DIM)
    kk = heads(k * k_k)
    kk = kk / jnp.maximum(jnp.linalg.norm(kk, axis=-1, keepdims=True), 1e-12)
    k = k * (1.0 + (a - 1.0) * k_a)
    r_h, k_h, v_h, a_h = heads(r), heads(k), heads(v), heads(a)
    y, S = wkv7_scan(r_h, heads(decay), k_h, v_h, kk, kk * a_h, wkv0.astype(F32))
    mean = jnp.mean(y, axis=-1, keepdims=True)
    var = jnp.mean(jnp.square(y - mean), axis=-1, keepdims=True)
    y = ((y - mean) * lax.rsqrt(var + LNX_EPS)).reshape(Bsz, L, A_WIDTH) * lnx_w + lnx_b
    y = y + (jnp.sum(r_h * k_h * r_k, axis=-1, keepdims=True) * v_h).reshape(Bsz, L, A_WIDTH)
    return y * g, za[:, -1], S


def ssd_chunked(x, dt, A, Bm, Cm, S0, chunk):
    Bsz, L = x.shape[:2]
    nc = L // chunk

    def blocks(t):
        return jnp.moveaxis(t.reshape((Bsz, nc, chunk) + t.shape[2:]), 1, 0)

    causal = jnp.tril(jnp.ones((chunk, chunk), dtype=bool))[None, :, :, None, None]

    def step(S, inp):
        xc, dtc, Bc, Cc = inp
        a_cs = jnp.cumsum(dtc * A, axis=1)
        seg = a_cs[:, :, None] - a_cs[:, None, :]
        decay_qs = jnp.exp(jnp.where(causal, seg, -jnp.inf))
        xdt = xc * dtc[..., None]
        cb = jnp.einsum('bqgn,bsgn->bqsg', Cc, Bc)
        y = jnp.einsum('bqsg,bqsgh,bsghp->bqghp', cb, decay_qs, xdt)
        y = y + jnp.einsum('bqgn,bghpn->bqghp', Cc, S) * jnp.exp(a_cs)[..., None]
        to_end = jnp.exp(a_cs[:, -1:] - a_cs)
        S = S * jnp.exp(a_cs[:, -1])[..., None, None] + jnp.einsum('bsgn,bsgh,bsghp->bghpn', Bc, to_end, xdt)
        return S, y

    S, ys = lax.scan(step, S0, (blocks(x), blocks(dt), blocks(Bm), blocks(Cm)))
    return jnp.moveaxis(ys, 0, 1).reshape(x.shape), S


def mamba2_mix(zb, conv0, ssm0, segments, conv_w, conv_b, dt_bias, A_log, D_skip, norm_w):
    Bsz, L, _ = zb.shape
    Hg = B_HEADS // B_GROUPS
    z, xbc, dt = jnp.split(zb, [B_WIDTH, B_WIDTH + B_CONV_DIM], axis=-1)
    xpad = jnp.concatenate([conv0.astype(F32), xbc], axis=1)
    taps = conv_w.astype(F32).T[:, None, :]
    conv = lax.conv_general_dilated(xpad, taps, (1,), 'VALID',
                                    dimension_numbers=('NWC', 'WIO', 'NWC'),
                                    feature_group_count=B_CONV_DIM)
    xbc = jax.nn.silu(conv + conv_b)
    xs, Bm, Cm = jnp.split(xbc, [B_WIDTH, B_WIDTH + B_GROUPS * D_STATE], axis=-1)
    xs = xs.reshape(Bsz, L, B_GROUPS, Hg, B_HEAD_DIM)
    Bm = Bm.reshape(Bsz, L, B_GROUPS, D_STATE)
    Cm = Cm.reshape(Bsz, L, B_GROUPS, D_STATE)
    dt = jax.nn.softplus(dt + dt_bias).reshape(Bsz, L, B_GROUPS, Hg)
    A = -jnp.exp(A_log.astype(F32)).reshape(B_GROUPS, Hg)
    S = ssm0.astype(F32).reshape(Bsz, B_GROUPS, Hg, B_HEAD_DIM, D_STATE)
    ys = []
    start = 0
    for seg_len, chunk in segments:
        sl = slice(start, start + seg_len)
        y_seg, S = ssd_chunked(xs[:, sl], dt[:, sl], A, Bm[:, sl], Cm[:, sl], S, chunk)
        ys.append(y_seg)
        start += seg_len
    y = jnp.concatenate(ys, axis=1) + xs * D_skip.reshape(B_GROUPS, Hg, 1)
    y = y.reshape(Bsz, L, B_WIDTH) * jax.nn.silu(z)
    yg = y.reshape(Bsz, L, B_GROUPS, B_WIDTH // B_GROUPS)
    yg = yg * lax.rsqrt(jnp.mean(yg * yg, axis=-1, keepdims=True) + EPS)
    y = yg.reshape(Bsz, L, B_WIDTH) * norm_w
    return y, xpad[:, -(CONV_W - 1):], S.reshape(Bsz, B_HEADS, B_HEAD_DIM, D_STATE)


def hybrid_layer(x, shift0, wkv0, conv0, ssm0, segments,
                 norm_mix_w, w_in, rwkv_mu, rwkv_w0, rwkv_w_up, rwkv_a0, rwkv_a_up, rwkv_g_up,
                 rwkv_k_k, rwkv_k_a, rwkv_r_k, rwkv_lnx_w, rwkv_lnx_b,
                 ssm_conv_w, ssm_conv_b, ssm_dt_bias, ssm_A_log, ssm_D, ssm_norm_w,
                 w_out, norm_ffn_w, ffn_w_gate, ffn_w_up, ffn_w_down):
    dtype = x.dtype
    h = rmsnorm(x, norm_mix_w)
    proj = (h @ w_in).astype(F32)
    ya, shift_new, wkv_new = rwkv7_mix(proj[..., :A_COLS], shift0, wkv0, rwkv_mu, rwkv_w0, rwkv_w_up,
                                       rwkv_a0, rwkv_a_up, rwkv_g_up, rwkv_k_k, rwkv_k_a, rwkv_r_k,
                                       rwkv_lnx_w, rwkv_lnx_b)
    yb, conv_new, ssm_new = mamba2_mix(proj[..., A_COLS:], conv0, ssm0, segments, ssm_conv_w, ssm_conv_b,
                                       ssm_dt_bias, ssm_A_log, ssm_D, ssm_norm_w)
    x = x + jnp.concatenate([ya, yb], axis=-1).astype(dtype) @ w_out
    h2 = rmsnorm(x, norm_ffn_w)
    x = x + (jax.nn.silu(h2 @ ffn_w_gate) * (h2 @ ffn_w_up)) @ ffn_w_down
    return (x, shift_new.astype(dtype), wkv_new.astype(dtype), conv_new.astype(dtype), ssm_new.astype(dtype))


def run_trunk(x, shift, wkv, conv, ssm, segments, layer_weights, norm_final_w):
    new_shift, new_wkv, new_conv, new_ssm = [], [], [], []
    for l in range(DEPTH):
        x, s1, s2, s3, s4 = hybrid_layer(x, shift[l], wkv[l], conv[l], ssm[l], segments,
                                         *[w[l] for w in layer_weights])
        new_shift.append(s1)
        new_wkv.append(s2)
        new_conv.append(s3)
        new_ssm.append(s4)
    return (rmsnorm(x, norm_final_w), jnp.stack(new_shift), jnp.stack(new_wkv),
            jnp.stack(new_conv), jnp.stack(new_ssm))


def setup_inputs(seed: int = 0) -> dict:
    key = jax.random.key(seed)
    k = jax.random.split(key, 32)
    nrm = lambda kk, shape, scale: scale * jax.random.normal(kk, shape, F32)
    unif = lambda kk, shape, lo, hi: jax.random.uniform(kk, shape, F32, lo, hi)
    dt_init = jnp.exp(unif(k[20], (DEPTH, B_HEADS), math.log(1e-3), math.log(1e-1)))
    return {
        'x_prompt': nrm(k[0], (BATCH, SEQ, D_MODEL), 1.0),
        'x_sample': nrm(k[1], (DEC_BATCH, DEC_SEQ, D_MODEL), 1.0),
        'state_rwkv_shift': nrm(k[2], (DEPTH, DEC_BATCH, A_COLS), 1.0),
        'state_rwkv_wkv': nrm(k[3], (DEPTH, DEC_BATCH, A_HEADS, A_HEAD_DIM, A_HEAD_DIM), 0.1),
        'state_ssm_conv': nrm(k[4], (DEPTH, DEC_BATCH, CONV_W - 1, B_CONV_DIM), 1.0),
        'state_ssm': nrm(k[5], (DEPTH, DEC_BATCH, B_HEADS, B_HEAD_DIM, D_STATE), 0.1),
        'meta_tokens': nrm(k[6], (N_META, D_MODEL), 1.0),
        'norm_mix_w': 1.0 + nrm(k[7], (DEPTH, D_MODEL), 0.02),
        'w_in': nrm(k[8], (DEPTH, D_MODEL, IN_COLS), D_MODEL ** -0.5),
        'rwkv_mu': unif(k[9], (DEPTH, A_COLS), 0.0, 1.0),
        'rwkv_w0': unif(k[10], (DEPTH, A_WIDTH), -5.5, -0.5),
        'rwkv_w_up': nrm(k[11], (DEPTH, DECAY_RANK, A_WIDTH), 0.5 * DECAY_RANK ** -0.5),
        'rwkv_a0': nrm(k[12], (DEPTH, A_WIDTH), 0.1),
        'rwkv_a_up': nrm(k[13], (DEPTH, AAA_RANK, A_WIDTH), AAA_RANK ** -0.5),
        'rwkv_g_up': nrm(k[14], (DEPTH, GATE_RANK, A_WIDTH), GATE_RANK ** -0.5),
        'rwkv_k_k': 0.85 + nrm(k[15], (DEPTH, A_WIDTH), 0.02),
        'rwkv_k_a': 1.0 + nrm(k[16], (DEPTH, A_WIDTH), 0.02),
        'rwkv_r_k': nrm(k[17], (DEPTH, A_HEADS, A_HEAD_DIM), 0.1),
        'rwkv_lnx_w': 1.0 + nrm(k[18], (DEPTH, A_WIDTH), 0.02),
        'rwkv_lnx_b': nrm(k[19], (DEPTH, A_WIDTH), 0.01),
        'ssm_conv_w': nrm(k[21], (DEPTH, B_CONV_DIM, CONV_W), CONV_W ** -0.5),
        'ssm_conv_b': nrm(k[22], (DEPTH, B_CONV_DIM), 0.01),
        'ssm_dt_bias': dt_init + jnp.log(-jnp.expm1(-dt_init)),
        'ssm_A_log': jnp.log(unif(k[23], (DEPTH, B_HEADS), 1.0, 16.0)),
        'ssm_D': 1.0 + nrm(k[24], (DEPTH, B_HEADS), 0.02),
        'ssm_norm_w': 1.0 + nrm(k[25], (DEPTH, B_WIDTH), 0.02),
        'w_out': nrm(k[26], (DEPTH, D_MIX, D_MODEL), D_MIX ** -0.5),
        'norm_ffn_w': 1.0 + nrm(k[27], (DEPTH, D_MODEL), 0.02),
        'ffn_w_gate': nrm(k[28], (DEPTH, D_MODEL, D_FF), D_MODEL ** -0.5),
        'ffn_w_up': nrm(k[29], (DEPTH, D_MODEL, D_FF), D_MODEL ** -0.5),
        'ffn_w_down': nrm(k[30], (DEPTH, D_FF, D_MODEL), D_FF ** -0.5),
        'norm_final_w': 1.0 + nrm(k[31], (D_MODEL,), 0.02),
    }


def reference(x_prompt, x_sample, state_rwkv_shift, state_rwkv_wkv, state_ssm_conv, state_ssm,
              meta_tokens, norm_mix_w, w_in, rwkv_mu, rwkv_w0, rwkv_w_up, rwkv_a0, rwkv_a_up, rwkv_g_up,
              rwkv_k_k, rwkv_k_a, rwkv_r_k, rwkv_lnx_w, rwkv_lnx_b,
              ssm_conv_w, ssm_conv_b, ssm_dt_bias, ssm_A_log, ssm_D, ssm_norm_w,
              w_out, norm_ffn_w, ffn_w_gate, ffn_w_up, ffn_w_down, norm_final_w):
    layer_weights = (norm_mix_w, w_in, rwkv_mu, rwkv_w0, rwkv_w_up, rwkv_a0, rwkv_a_up, rwkv_g_up,
                     rwkv_k_k, rwkv_k_a, rwkv_r_k, rwkv_lnx_w, rwkv_lnx_b,
                     ssm_conv_w, ssm_conv_b, ssm_dt_bias, ssm_A_log, ssm_D, ssm_norm_w,
                     w_out, norm_ffn_w, ffn_w_gate, ffn_w_up, ffn_w_down)
    Bp, Lp = x_prompt.shape[:2]
    dtype = x_prompt.dtype
    meta = jnp.broadcast_to(meta_tokens.astype(dtype)[None], (Bp, N_META, D_MODEL))
    x_p = jnp.concatenate([meta, x_prompt], axis=1)
    zeros = lambda *shape: jnp.zeros(shape, dtype)
    prompt_segments = ((N_META, N_META), (Lp, min(CHUNK, Lp)))
    y_p, p_shift, p_wkv, p_conv, p_ssm = run_trunk(
        x_p, zeros(DEPTH, Bp, A_COLS), zeros(DEPTH, Bp, A_HEADS, A_HEAD_DIM, A_HEAD_DIM),
        zeros(DEPTH, Bp, CONV_W - 1, B_CONV_DIM), zeros(DEPTH, Bp, B_HEADS, B_HEAD_DIM, D_STATE),
        prompt_segments, layer_weights, norm_final_w)
    y_prompt = y_p[:, N_META:]
    Ls = x_sample.shape[1]
    y_sample, s_shift, s_wkv, s_conv, s_ssm = run_trunk(
        x_sample, state_rwkv_shift, state_rwkv_wkv, state_ssm_conv, state_ssm,
        ((Ls, min(CHUNK, Ls)),), layer_weights, norm_final_w)
    return (y_prompt, y_sample, p_shift, p_wkv, p_conv, p_ssm, s_shift, s_wkv, s_conv, s_ssm)
```

```python
import functools
import math

import jax
import jax.numpy as jnp
from jax import lax
from jax.experimental import pallas as pl
from jax.experimental.pallas import tpu as pltpu

F32 = jnp.float32
BF16 = jnp.bfloat16
HI = lax.Precision.HIGHEST

D_MODEL = 2048
CHUNK = 64
EPS = 1e-6
LNX_EPS = 64e-5
A_WIDTH = 1024
A_HEADS = 16
A_HEAD_DIM = 64
DECAY_RANK = 64
AAA_RANK = 64
GATE_RANK = 160
B_WIDTH = 1024
B_HEADS = 16
B_HEAD_DIM = 64
B_GROUPS = 2
D_STATE = 128
CONV_W = 4
BC_WIDTH = 2 * B_GROUPS * D_STATE
A_COLS = 3 * A_WIDTH + DECAY_RANK + AAA_RANK + GATE_RANK
LOWRANK = DECAY_RANK + AAA_RANK + GATE_RANK
D_FF = 5632

LANES = 128
SUBLANES = 8
VMEM_LIMIT = 56 * 1024 * 1024

LR_PAD = 512
DT_PAD = 128
COL_R, COL_K, COL_V, COL_Z, COL_X = 0, 1024, 2048, 3072, 4096
COL_BC = 5120
COL_LR = 5632
COL_DT = 6144
PROJ_COLS = COL_DT + DT_PAD
PROJ_TN = 896
CONV_ROWS = SUBLANES


def _cparams(sem):
    return pltpu.CompilerParams(dimension_semantics=sem, vmem_limit_bytes=VMEM_LIMIT)


def _proj_kernel(x_ref, nw_ref, w_ref, o_ref, h_ref):
    @pl.when(pl.program_id(1) == 0)
    def _():
        x = x_ref[...]
        ms = jnp.mean(x * x, axis=-1, keepdims=True)
        h_ref[...] = (x * lax.rsqrt(ms + EPS) * nw_ref[...]).astype(BF16)

    o_ref[...] = jnp.dot(h_ref[...], w_ref[...], preferred_element_type=F32)


def _proj(x, norm_w, w, tm):
    m = x.shape[0]
    return pl.pallas_call(
        _proj_kernel,
        grid=(m // tm, PROJ_COLS // PROJ_TN),
        in_specs=[
            pl.BlockSpec((tm, D_MODEL), lambda i, j: (i, 0)),
            pl.BlockSpec((1, D_MODEL), lambda i, j: (0, 0)),
            pl.BlockSpec((D_MODEL, PROJ_TN), lambda i, j: (0, j)),
        ],
        out_specs=pl.BlockSpec((tm, PROJ_TN), lambda i, j: (i, j)),
        out_shape=jax.ShapeDtypeStruct((m, PROJ_COLS), F32),
        scratch_shapes=[pltpu.VMEM((tm, D_MODEL), BF16)],
        compiler_params=_cparams(("parallel", "arbitrary")),
        name="proj",
    )(x, norm_w, w)


def _outproj_kernel(x_ref, ya_ref, yb_ref, wa_ref, wb_ref, nw_ref, x1_ref, h2_ref):
    x1 = (x_ref[...]
          + jnp.dot(ya_ref[...], wa_ref[...], preferred_element_type=F32)
          + jnp.dot(yb_ref[...], wb_ref[...], preferred_element_type=F32))
    x1_ref[...] = x1
    ms = jnp.mean(x1 * x1, axis=-1, keepdims=True)
    h2_ref[...] = (x1 * lax.rsqrt(ms + EPS) * nw_ref[...]).astype(BF16)


def _outproj(x, ya, yb, wa, wb, norm_w, tm):
    m = x.shape[0]
    row = lambda i: (i, 0)
    fixed = lambda i: (0, 0)
    return pl.pallas_call(
        _outproj_kernel,
        grid=(m // tm,),
        in_specs=[
            pl.BlockSpec((tm, D_MODEL), row),
            pl.BlockSpec((tm, A_WIDTH), row),
            pl.BlockSpec((tm, B_WIDTH), row),
            pl.BlockSpec((A_WIDTH, D_MODEL), fixed),
            pl.BlockSpec((B_WIDTH, D_MODEL), fixed),
            pl.BlockSpec((1, D_MODEL), fixed),
        ],
        out_specs=[pl.BlockSpec((tm, D_MODEL), row), pl.BlockSpec((tm, D_MODEL), row)],
        out_shape=[jax.ShapeDtypeStruct((m, D_MODEL), F32),
                   jax.ShapeDtypeStruct((m, D_MODEL), BF16)],
        compiler_params=_cparams(("parallel",)),
        name="outproj",
    )(x, ya, yb, wa, wb, norm_w)


def _ffn1_kernel(h_ref, wg_ref, wu_ref, o_ref):
    h = h_ref[...]
    g = jnp.dot(h, wg_ref[...], preferred_element_type=F32)
    u = jnp.dot(h, wu_ref[...], preferred_element_type=F32)
    o_ref[...] = (g * jax.nn.sigmoid(g) * u).astype(BF16)


def _ffn1(h2, wg, wu, tm, tn):
    m = h2.shape[0]
    return pl.pallas_call(
        _ffn1_kernel,
        grid=(m // tm, D_FF // tn),
        in_specs=[
            pl.BlockSpec((tm, D_MODEL), lambda i, j: (i, 0)),
            pl.BlockSpec((D_MODEL, tn), lambda i, j: (0, j)),
            pl.BlockSpec((D_MODEL, tn), lambda i, j: (0, j)),
        ],
        out_specs=pl.BlockSpec((tm, tn), lambda i, j: (i, j)),
        out_shape=jax.ShapeDtypeStruct((m, D_FF), BF16),
        compiler_params=_cparams(("parallel", "parallel")),
        name="ffn1",
    )(h2, wg, wu)


def _ffn2_kernel(g_ref, wd_ref, x1_ref, nw_ref, o_ref, acc_ref):
    k = pl.program_id(1)

    @pl.when(k == 0)
    def _():
        acc_ref[...] = x1_ref[...]

    acc_ref[...] += jnp.dot(g_ref[...], wd_ref[...], preferred_element_type=F32)

    @pl.when(k == pl.num_programs(1) - 1)
    def _():
        x2 = acc_ref[...]
        ms = jnp.mean(x2 * x2, axis=-1, keepdims=True)
        o_ref[...] = x2 * lax.rsqrt(ms + EPS) * nw_ref[...]


def _ffn2(g, wd, x1, norm_w, tm, tk):
    m = g.shape[0]
    return pl.pallas_call(
        _ffn2_kernel,
        grid=(m // tm, D_FF // tk),
        in_specs=[
            pl.BlockSpec((tm, tk), lambda i, k: (i, k)),
            pl.BlockSpec((tk, D_MODEL), lambda i, k: (k, 0)),
            pl.BlockSpec((tm, D_MODEL), lambda i, k: (i, 0)),
            pl.BlockSpec((1, D_MODEL), lambda i, k: (0, 0)),
        ],
        out_specs=pl.BlockSpec((tm, D_MODEL), lambda i, k: (i, 0)),
        out_shape=jax.ShapeDtypeStruct((m, D_MODEL), F32),
        scratch_shapes=[pltpu.VMEM((tm, D_MODEL), F32)],
        compiler_params=_cparams(("parallel", "arbitrary")),
        name="ffn2",
    )(g, wd, x1, norm_w)


def _shift_rows(z, prev_row):
    rolled = pltpu.roll(z, 1, axis=0)
    row = lax.broadcasted_iota(jnp.int32, z.shape, 0)
    return jnp.where(row == 0, prev_row, rolled)


def _rwkv_kernel(zr_ref, zk_ref, zv_ref, zl_ref, sh_rkv_ref, sh_l_ref, s0_ref,
                 mu_rkv_ref, mu_l_ref, w0_ref, a0_ref, wa_up_ref, g_up_ref,
                 kk_ref, ka_ref, rk_ref, lnw_ref, lnb_ref, e_ref, et_ref,
                 ya_ref, osh_rkv_ref, osh_l_ref, os_ref,
                 prev_ref, s_ref, stk_ref, plast_ref, *, chunk):
    C = chunk
    W = A_WIDTH
    c_idx = pl.program_id(1)

    @pl.when(c_idx == 0)
    def _():
        prev_ref[:, 0:3 * W] = sh_rkv_ref[0]
        prev_ref[:, 3 * W:3 * W + LR_PAD] = sh_l_ref[0]
        s_ref[...] = s0_ref[0]

    def hsum(x):
        s = jnp.dot(x, e_ref[...], precision=HI, preferred_element_type=F32)
        return jnp.dot(s, et_ref[...], precision=HI, preferred_element_type=F32)

    def mixed(z_ref, lo, width, mu):
        z = z_ref[...]
        zp = _shift_rows(z, prev_ref[:, lo:lo + width])
        prev_ref[:, lo:lo + width] = z[C - 1:C, :]
        return z + mu * (zp - z)

    r = mixed(zr_ref, 0, W, mu_rkv_ref[:, 0:W])
    k = mixed(zk_ref, W, W, mu_rkv_ref[:, W:2 * W])
    v = mixed(zv_ref, 2 * W, W, mu_rkv_ref[:, 2 * W:3 * W])
    zl = mixed(zl_ref, 3 * W, LR_PAD, mu_l_ref[...])
    osh_rkv_ref[0] = prev_ref[:, 0:3 * W]
    osh_l_ref[0] = prev_ref[:, 3 * W:3 * W + LR_PAD]

    l1 = zl[:, 0:LANES]
    lane = lax.broadcasted_iota(jnp.int32, l1.shape, 1)
    l1 = jnp.where(lane < DECAY_RANK, jnp.tanh(l1), l1)
    wa = jnp.dot(l1.astype(BF16), wa_up_ref[...], preferred_element_type=F32)
    gl = jax.nn.sigmoid(zl[:, LANES:3 * LANES])
    g = jnp.dot(gl.astype(BF16), g_up_ref[...], preferred_element_type=F32)

    wlog = -jax.nn.softplus(-(w0_ref[...] + wa[:, 0:W])) - 0.5
    lw = -jnp.exp(wlog)
    a = jax.nn.sigmoid(a0_ref[...] + wa[:, W:2 * W])
    kk = k * kk_ref[...]
    kk = kk / jnp.maximum(jnp.sqrt(hsum(kk * kk)), 1e-12)
    k2 = k * (1.0 + (a - 1.0) * ka_ref[...])
    ka = kk * a

    ti = lax.broadcasted_iota(jnp.int32, (C, C), 0)
    si = lax.broadcasted_iota(jnp.int32, (C, C), 1)
    incl = si <= ti
    strict = si < ti
    cum = jnp.dot(incl.astype(F32), lw, precision=HI, preferred_element_type=F32)
    cum_last = cum[C - 1:C, :]
    e_neg = jnp.exp(-cum)
    e_end = jnp.exp(cum_last - cum)
    stk_ref[0] = kk * jnp.exp(cum - lw)
    stk_ref[1] = r * jnp.exp(cum)
    stk_ref[2] = k2 * e_neg
    stk_ref[3] = ka * e_neg
    stk_ref[4] = k2 * e_end
    stk_ref[5] = ka * e_end
    stk_ref[6] = v
    plast_ref[...] = jnp.exp(cum_last)

    eye = si == ti
    eye_head = (lax.broadcasted_iota(jnp.int32, (A_HEAD_DIM, A_HEAD_DIM), 0)
                == lax.broadcasted_iota(jnp.int32, (A_HEAD_DIM, A_HEAD_DIM), 1))
    n_sq = int(math.log2(C)) - 1

    def dot_hi(x, y):
        return jnp.dot(x, y, precision=HI, preferred_element_type=F32)

    def pair_body(p, carry):
        off = pl.multiple_of(p * LANES, LANES)
        tiles = [stk_ref[i, :, pl.ds(off, LANES)] for i in range(7)]
        p_last = plast_ref[:, pl.ds(off, LANES)]
        ys = []
        for j in range(LANES // A_HEAD_DIM):
            sl = slice(j * A_HEAD_DIM, (j + 1) * A_HEAD_DIM)
            at, rt, kh, bh, ke, be, vh = [t[:, sl] for t in tiles]
            s0 = s_ref[2 * p + j]
            lhs = jnp.concatenate([at, rt], axis=0)
            rhs = jnp.concatenate([kh, bh], axis=0)
            m1 = lax.dot_general(lhs, rhs, (((1,), (1,)), ((), ())),
                                 precision=HI, preferred_element_type=F32)
            a_k = jnp.where(strict, m1[0:C, 0:C], 0.0)
            a_b = jnp.where(strict, m1[0:C, C:2 * C], 0.0)
            b_k = jnp.where(incl, m1[C:2 * C, 0:C], 0.0)
            b_b = jnp.where(incl, m1[C:2 * C, C:2 * C], 0.0)
            m2 = dot_hi(lhs, s0)
            rhs_u = m2[0:C] + dot_hi(a_k, vh)
            t_inv = jnp.where(eye, 1.0, 0.0) - a_b
            pw = a_b
            for _ in range(n_sq):
                pw = dot_hi(pw, pw)
                t_inv = t_inv + dot_hi(t_inv, pw)
            u = dot_hi(t_inv, rhs_u)
            vu = jnp.concatenate([vh, u], axis=0)
            y = m2[C:2 * C] + dot_hi(jnp.concatenate([b_k, -b_b], axis=1), vu)
            ys.append(y)
            diag_p = jnp.where(eye_head, p_last[:, sl], 0.0)
            upd_l = jnp.concatenate([ke, -be, diag_p], axis=0)
            upd_r = jnp.concatenate([vu, s0], axis=0)
            s_ref[2 * p + j] = lax.dot_general(upd_l, upd_r, (((0,), (0,)), ((), ())),
                                               precision=HI, preferred_element_type=F32)
        stk_ref[7, :, pl.ds(off, LANES)] = jnp.concatenate(ys, axis=1)
        return carry

    lax.fori_loop(0, W // LANES, pair_body, 0)

    y = stk_ref[7]
    inv_n = 1.0 / A_HEAD_DIM
    mean = hsum(y) * inv_n
    d = y - mean
    var = hsum(d * d) * inv_n
    y = d * lax.rsqrt(var + LNX_EPS) * lnw_ref[...] + lnb_ref[...]
    y = y + hsum(r * k2 * rk_ref[...]) * v
    ya_ref[...] = (y * g).astype(BF16)

    @pl.when(c_idx == pl.num_programs(1) - 1)
    def _():
        os_ref[0] = s_ref[...]


def _rwkv(proj, row_off, nb, nc, chunk, sh_rkv, sh_l, s0, shared_state, wts):
    C = chunk
    rb = row_off // C
    W = A_WIDTH
    zspec = lambda width, col: pl.BlockSpec((C, width), lambda b, c: (rb + b * nc + c, col // width))
    st = (lambda b: 0) if shared_state else (lambda b: b)
    fixed2 = lambda b, c: (0, 0)
    full = lambda arr: pl.BlockSpec(arr.shape, fixed2)
    in_specs = [
        zspec(W, COL_R), zspec(W, COL_K), zspec(W, COL_V), zspec(LR_PAD, COL_LR),
        pl.BlockSpec((1, 1, 3 * W), lambda b, c: (st(b), 0, 0)),
        pl.BlockSpec((1, 1, LR_PAD), lambda b, c: (st(b), 0, 0)),
        pl.BlockSpec((1, A_HEADS, A_HEAD_DIM, A_HEAD_DIM), lambda b, c: (st(b), 0, 0, 0)),
    ] + [full(w) for w in wts]
    out_specs = [
        pl.BlockSpec((C, W), lambda b, c: (b * nc + c, 0)),
        pl.BlockSpec((1, 1, 3 * W), lambda b, c: (b, 0, 0)),
        pl.BlockSpec((1, 1, LR_PAD), lambda b, c: (b, 0, 0)),
        pl.BlockSpec((1, A_HEADS, A_HEAD_DIM, A_HEAD_DIM), lambda b, c: (b, 0, 0, 0)),
    ]
    out_shape = [
        jax.ShapeDtypeStruct((nb * nc * C, W), BF16),
        jax.ShapeDtypeStruct((nb, 1, 3 * W), F32),
        jax.ShapeDtypeStruct((nb, 1, LR_PAD), F32),
        jax.ShapeDtypeStruct((nb, A_HEADS, A_HEAD_DIM, A_HEAD_DIM), F32),
    ]
    return pl.pallas_call(
        functools.partial(_rwkv_kernel, chunk=C),
        grid=(nb, nc),
        in_specs=in_specs,
        out_specs=out_specs,
        out_shape=out_shape,
        scratch_shapes=[
            pltpu.VMEM((1, 3 * W + LR_PAD), F32),
            pltpu.VMEM((A_HEADS, A_HEAD_DIM, A_HEAD_DIM), F32),
            pltpu.VMEM((8, C, W), F32),
            pltpu.VMEM((1, W), F32),
        ],
        compiler_params=_cparams(("parallel", "arbitrary")),
        name="rwkv",
    )(proj, proj, proj, proj, sh_rkv, sh_l, s0, *wts)


def _mamba_kernel(z_ref, x_ref, bc_ref, dt_ref, cx0_ref, cbc0_ref, s0_ref,
                  cwx_ref, cwbc_ref, cbx_ref, cbbc_ref, dtb_ref, alog_ref, dsk_ref, nw_ref, e_ref,
                  yb_ref, ocx_ref, ocbc_ref, os_ref,
                  cx_ref, cbc_ref, s_ref, y_ref, *, chunk):
    C = chunk
    c_idx = pl.program_id(1)
    GW = B_WIDTH // B_GROUPS
    HG = B_HEADS // B_GROUPS

    @pl.when(c_idx == 0)
    def _():
        cx_ref[...] = cx0_ref[0]
        cbc_ref[...] = cbc0_ref[0]
        s_ref[...] = s0_ref[0]

    def causal_conv(raw_ref, carry_ref, taps_ref, bias_ref):
        raw = raw_ref[...]
        ext = jnp.concatenate([carry_ref[...], raw], axis=0)
        acc = bias_ref[...]
        for j in range(CONV_W):
            shifted = ext if j == CONV_W - 1 else pltpu.roll(ext, CONV_W - 1 - j, axis=0)
            acc = acc + taps_ref[j:j + 1, :] * shifted[CONV_ROWS:CONV_ROWS + C, :]
        carry_ref[...] = raw[C - CONV_ROWS:C, :]
        return acc * jax.nn.sigmoid(acc)

    xs = causal_conv(x_ref, cx_ref, cwx_ref, cbx_ref)
    bcm = causal_conv(bc_ref, cbc_ref, cwbc_ref, cbbc_ref)
    ocx_ref[0] = cx_ref[...]
    ocbc_ref[0] = cbc_ref[...]

    dt = jax.nn.softplus(dt_ref[...] + dtb_ref[...])
    d_a = dt * (-jnp.exp(alog_ref[...]))
    ti = lax.broadcasted_iota(jnp.int32, (C, C), 0)
    si = lax.broadcasted_iota(jnp.int32, (C, C), 1)
    causal = si <= ti
    a_cs = jnp.dot(causal.astype(F32), d_a, precision=HI, preferred_element_type=F32)
    expand = lambda t: jnp.dot(t, e_ref[...], precision=HI, preferred_element_type=F32)
    dt_e = expand(dt)
    acs_e = expand(a_cs)
    alast_e = acs_e[C - 1:C, :]
    acs_t = a_cs.T

    xdt = xs * dt_e
    xdt_b = xdt.astype(BF16)
    xend_b = (xdt * jnp.exp(alast_e - acs_e)).astype(BF16)
    s_decay = jnp.exp(alast_e)
    in_decay = jnp.exp(acs_e)

    for g in range(B_GROUPS):
        bm = bcm[:, g * D_STATE:(g + 1) * D_STATE].astype(BF16)
        cm = bcm[:, (B_GROUPS + g) * D_STATE:(B_GROUPS + g + 1) * D_STATE].astype(BF16)
        cb = lax.dot_general(cm, bm, (((1,), (1,)), ((), ())), preferred_element_type=F32)
        gsl = slice(g * GW, (g + 1) * GW)
        s_old = s_ref[:, gsl]
        y_state = jnp.dot(cm, s_old.astype(BF16), preferred_element_type=F32) * in_decay[:, gsl]
        for hh in range(HG):
            h = g * HG + hh
            hsl = slice(h * B_HEAD_DIM, (h + 1) * B_HEAD_DIM)
            seg = acs_e[:, h * B_HEAD_DIM:h * B_HEAD_DIM + C] - acs_t[h:h + 1, :]
            m = jnp.where(causal, jnp.exp(seg), 0.0) * cb
            y_ref[:, hsl] = jnp.dot(m.astype(BF16), xdt_b[:, hsl], preferred_element_type=F32)
        y_ref[:, gsl] = y_ref[:, gsl] + y_state
        s_ref[:, gsl] = s_old * s_decay[:, gsl] + lax.dot_general(
            bm, xend_b[:, gsl], (((0,), (0,)), ((), ())), preferred_element_type=F32)

    z = z_ref[...]
    y = (y_ref[...] + xs * dsk_ref[...]) * (z * jax.nn.sigmoid(z))
    outs = []
    for g in range(B_GROUPS):
        yg = y[:, g * GW:(g + 1) * GW]
        ms = jnp.mean(yg * yg, axis=-1, keepdims=True)
        outs.append(yg * lax.rsqrt(ms + EPS))
    yb_ref[...] = (jnp.concatenate(outs, axis=1) * nw_ref[...]).astype(BF16)

    @pl.when(c_idx == pl.num_programs(1) - 1)
    def _():
        os_ref[0] = s_ref[...]


def _mamba(proj, row_off, nb, nc, chunk, cx0, cbc0, s0, shared_state, wts):
    C = chunk
    assert C >= CONV_ROWS and C <= B_HEAD_DIM
    rb = row_off // C
    W = B_WIDTH
    zspec = lambda width, col: pl.BlockSpec((C, width), lambda b, c: (rb + b * nc + c, col // width))
    st = (lambda b: 0) if shared_state else (lambda b: b)
    full = lambda arr: pl.BlockSpec(arr.shape, lambda b, c: (0, 0))
    in_specs = [
        zspec(W, COL_Z), zspec(W, COL_X), zspec(BC_WIDTH, COL_BC), zspec(DT_PAD, COL_DT),
        pl.BlockSpec((1, CONV_ROWS, W), lambda b, c: (st(b), 0, 0)),
        pl.BlockSpec((1, CONV_ROWS, BC_WIDTH), lambda b, c: (st(b), 0, 0)),
        pl.BlockSpec((1, D_STATE, W), lambda b, c: (st(b), 0, 0)),
    ] + [full(w) for w in wts]
    out_specs = [
        pl.BlockSpec((C, W), lambda b, c: (b * nc + c, 0)),
        pl.BlockSpec((1, CONV_ROWS, W), lambda b, c: (b, 0, 0)),
        pl.BlockSpec((1, CONV_ROWS, BC_WIDTH), lambda b, c: (b, 0, 0)),
        pl.BlockSpec((1, D_STATE, W), lambda b, c: (b, 0, 0)),
    ]
    out_shape = [
        jax.ShapeDtypeStruct((nb * nc * C, W), BF16),
        jax.ShapeDtypeStruct((nb, CONV_ROWS, W), F32),
        jax.ShapeDtypeStruct((nb, CONV_ROWS, BC_WIDTH), F32),
        jax.ShapeDtypeStruct((nb, D_STATE, W), F32),
    ]
    return pl.pallas_call(
        functools.partial(_mamba_kernel, chunk=C),
        grid=(nb, nc),
        in_specs=in_specs,
        out_specs=out_specs,
        out_shape=out_shape,
        scratch_shapes=[
            pltpu.VMEM((CONV_ROWS, W), F32),
            pltpu.VMEM((CONV_ROWS, BC_WIDTH), F32),
            pltpu.VMEM((D_STATE, W), F32),
            pltpu.VMEM((C, W), F32),
        ],
        compiler_params=_cparams(("parallel", "arbitrary")),
        name="mamba",
    )(proj, proj, proj, proj, cx0, cbc0, s0, *wts)


def _pad_cols(a, width):
    return jnp.pad(a, [(0, 0)] * (a.ndim - 1) + [(0, width - a.shape[-1])])


def _head_indicator(n_heads, head_dim, rows):
    col_head = jnp.arange(n_heads * head_dim, dtype=jnp.int32) // head_dim
    return (jnp.arange(rows, dtype=jnp.int32)[:, None] == col_head[None, :]).astype(F32)


def kernel(x_prompt, x_sample, state_rwkv_shift, state_rwkv_wkv, state_ssm_conv, state_ssm, meta_tokens, norm_mix_w, w_in, rwkv_mu, rwkv_w0, rwkv_w_up, rwkv_a0, rwkv_a_up, rwkv_g_up, rwkv_k_k, rwkv_k_a, rwkv_r_k, rwkv_lnx_w, rwkv_lnx_b, ssm_conv_w, ssm_conv_b, ssm_dt_bias, ssm_A_log, ssm_D, ssm_norm_w, w_out, norm_ffn_w, ffn_w_gate, ffn_w_up, ffn_w_down, norm_final_w):
    depth = w_in.shape[0]
    assert depth == 1, "single-layer step"
    bp, lp, d = x_prompt.shape
    bs, ls, _ = x_sample.shape
    n_meta = meta_tokens.shape[0]
    assert d == D_MODEL and lp % CHUNK == 0 and ls <= CHUNK and n_meta <= CHUNK
    assert w_in.shape[-1] == A_COLS + 2 * B_WIDTH + BC_WIDTH + B_HEADS
    W = A_WIDTH

    wi = w_in[0]
    w_perm = jnp.concatenate([
        wi[:, 0:3 * W],
        wi[:, A_COLS:A_COLS + 2 * B_WIDTH + BC_WIDTH],
        _pad_cols(wi[:, 3 * W:A_COLS], LR_PAD),
        _pad_cols(wi[:, A_COLS + 2 * B_WIDTH + BC_WIDTH:], DT_PAD),
    ], axis=1).astype(BF16)
    nmw = norm_mix_w.reshape(1, D_MODEL)
    mu = rwkv_mu[0]
    mu_rkv = mu[None, 0:3 * W]
    mu_l = _pad_cols(mu[None, 3 * W:], LR_PAD)
    wa_up = jnp.zeros((LANES, 2 * W), F32)
    wa_up = wa_up.at[0:DECAY_RANK, 0:W].set(rwkv_w_up[0])
    wa_up = wa_up.at[DECAY_RANK:DECAY_RANK + AAA_RANK, W:].set(rwkv_a_up[0]).astype(BF16)
    g_up = jnp.pad(rwkv_g_up[0], ((0, 2 * LANES - GATE_RANK), (0, 0))).astype(BF16)
    e_a = _head_indicator(A_HEADS, A_HEAD_DIM, LANES)
    rwkv_wts = (mu_rkv, mu_l, rwkv_w0.reshape(1, W), rwkv_a0.reshape(1, W), wa_up, g_up,
                rwkv_k_k.reshape(1, W), rwkv_k_a.reshape(1, W), rwkv_r_k.reshape(1, W),
                rwkv_lnx_w.reshape(1, W), rwkv_lnx_b.reshape(1, W), e_a.T, e_a)
    taps = ssm_conv_w[0].T
    cbias = ssm_conv_b.reshape(1, -1)
    e_b = _head_indicator(B_HEADS, B_HEAD_DIM, LANES)
    mamba_wts = (taps[:, :B_WIDTH], taps[:, B_WIDTH:], cbias[:, :B_WIDTH], cbias[:, B_WIDTH:],
                 _pad_cols(ssm_dt_bias.reshape(1, B_HEADS), DT_PAD),
                 _pad_cols(ssm_A_log.reshape(1, B_HEADS), DT_PAD),
                 jnp.repeat(ssm_D.reshape(1, B_HEADS), B_HEAD_DIM, axis=1),
                 ssm_norm_w.reshape(1, B_WIDTH), e_b)
    wo = w_out[0].astype(BF16)
    wo_a, wo_b = wo[:W], wo[W:]
    nfw = norm_ffn_w.reshape(1, D_MODEL)
    wg = ffn_w_gate[0].astype(BF16)
    wu = ffn_w_up[0].astype(BF16)
    wd = ffn_w_down[0].astype(BF16)
    nlw = norm_final_w.reshape(1, D_MODEL)

    tm_small = 384
    n_small = bs * ls + n_meta
    m_small = -(-n_small // tm_small) * tm_small
    x_small = jnp.concatenate([x_sample.reshape(bs * ls, d), meta_tokens,
                               jnp.zeros((m_small - n_small, d), F32)], axis=0)
    x_big = x_prompt.reshape(bp * lp, d)
    tm_big = math.gcd(bp * lp, 512)

    proj_small = _proj(x_small, nmw, w_perm, tm_small)
    proj_big = _proj(x_big, nmw, w_perm, tm_big)

    zeros = lambda *s: jnp.zeros(s, F32)
    sh = state_rwkv_shift[0]
    smp_rwkv = (sh[:, None, 0:3 * W], _pad_cols(sh[:, None, 3 * W:], LR_PAD),
                jnp.swapaxes(state_rwkv_wkv[0], -1, -2))
    conv0 = jnp.pad(state_ssm_conv[0], ((0, 0), (CONV_ROWS - (CONV_W - 1), 0), (0, 0)))
    ssm0 = jnp.transpose(state_ssm[0], (0, 3, 1, 2)).reshape(bs, D_STATE, B_WIDTH)
    smp_mamba = (conv0[..., :B_WIDTH], conv0[..., B_WIDTH:], ssm0)
    meta_rwkv0 = (zeros(1, 1, 3 * W), zeros(1, 1, LR_PAD), zeros(1, A_HEADS, A_HEAD_DIM, A_HEAD_DIM))
    meta_mamba0 = (zeros(1, CONV_ROWS, B_WIDTH), zeros(1, CONV_ROWS, BC_WIDTH), zeros(1, D_STATE, B_WIDTH))

    ya_m, *meta_rwkv = _rwkv(proj_small, bs * ls, 1, 1, n_meta, *meta_rwkv0, False, rwkv_wts)
    yb_m, *meta_mamba = _mamba(proj_small, bs * ls, 1, 1, n_meta, *meta_mamba0, False, mamba_wts)
    ya_s, *s_rwkv = _rwkv(proj_small, 0, bs, 1, ls, *smp_rwkv, False, rwkv_wts)
    yb_s, *s_mamba = _mamba(proj_small, 0, bs, 1, ls, *smp_mamba, False, mamba_wts)
    ya_p, *p_rwkv = _rwkv(proj_big, 0, bp, lp // CHUNK, CHUNK, *meta_rwkv, True, rwkv_wts)
    yb_p, *p_mamba = _mamba(proj_big, 0, bp, lp // CHUNK, CHUNK, *meta_mamba, True, mamba_wts)

    def tail(ya, yb, x, tm):
        x1, h2 = _outproj(x, ya, yb, wo_a, wo_b, nfw, tm)
        g = _ffn1(h2, wg, wu, tm, 512)
        return _ffn2(g, wd, x1, nlw, tm, D_FF // 4)

    pad_rows = lambda *parts: jnp.concatenate(
        list(parts) + [jnp.zeros((m_small - n_small, parts[0].shape[1]), parts[0].dtype)], axis=0)
    y_small = tail(pad_rows(ya_s, ya_m), pad_rows(yb_s, yb_m), x_small, tm_small)
    y_big = tail(ya_p, yb_p, x_big, tm_big)

    def states_out(rw, mb, nb):
        sh_rkv, sh_l, wkv = rw
        cx, cbc, ssm = mb
        shift = jnp.concatenate([sh_rkv[:, 0], sh_l[:, 0, :LOWRANK]], axis=-1)[None]
        wkv = jnp.swapaxes(wkv, -1, -2)[None]
        conv = jnp.concatenate([cx, cbc], axis=-1)[None, :, CONV_ROWS - (CONV_W - 1):]
        ssm = jnp.transpose(ssm.reshape(nb, D_STATE, B_HEADS, B_HEAD_DIM), (0, 2, 3, 1))[None]
        return shift, wkv, conv, ssm

    y_prompt = y_big.reshape(bp, lp, d)
    y_sample = y_small[:bs * ls].reshape(bs, ls, d)
    return (y_prompt, y_sample) + states_out(p_rwkv, p_mamba, bp) + states_out(s_rwkv, s_mamba, bs)
```

```python
import functools
import math

import jax
import jax.numpy as jnp
from jax import lax
from jax.experimental import pallas as pl
from jax.experimental.pallas import tpu as pltpu

F32 = jnp.float32
BF16 = jnp.bfloat16
HI = lax.Precision.HIGHEST

D_MODEL = 2048
CHUNK = 64
EPS = 1e-6
LNX_EPS = 64e-5
A_WIDTH = 1024
A_HEADS = 16
A_HEAD_DIM = 64
A_PAIRS = A_HEADS // 2
DECAY_RANK = 64
AAA_RANK = 64
GATE_RANK = 160
B_WIDTH = 1024
B_HEADS = 16
B_HEAD_DIM = 64
B_GROUPS = 2
D_STATE = 128
CONV_W = 4
BC_WIDTH = 2 * B_GROUPS * D_STATE
A_COLS = 3 * A_WIDTH + DECAY_RANK + AAA_RANK + GATE_RANK
LOWRANK = DECAY_RANK + AAA_RANK + GATE_RANK
D_FF = 5632

LANES = 128
SUBLANES = 8
VMEM_LIMIT = 56 * 1024 * 1024

LR_PAD = 512
DT_PAD = 128
COL_R, COL_K, COL_V, COL_Z, COL_X = 0, 1024, 2048, 3072, 4096
COL_BC = 5120
COL_LR = 5632
COL_DT = 6144
PROJ_COLS = COL_DT + DT_PAD
PROJ_TN = 896
CONV_ROWS = SUBLANES


def _cparams(sem):
    return pltpu.CompilerParams(dimension_semantics=sem, vmem_limit_bytes=VMEM_LIMIT)


def _proj_kernel(x_ref, nw_ref, w_ref, o_ref, h_ref):
    @pl.when(pl.program_id(1) == 0)
    def _():
        x = x_ref[...]
        ms = jnp.mean(x * x, axis=-1, keepdims=True)
        h_ref[...] = (x * lax.rsqrt(ms + EPS) * nw_ref[...]).astype(BF16)

    o_ref[...] = jnp.dot(h_ref[...], w_ref[...], preferred_element_type=F32)


def _proj(x, norm_w, w, tm):
    m = x.shape[0]
    return pl.pallas_call(
        _proj_kernel,
        grid=(m // tm, PROJ_COLS // PROJ_TN),
        in_specs=[
            pl.BlockSpec((tm, D_MODEL), lambda i, j: (i, 0)),
            pl.BlockSpec((1, D_MODEL), lambda i, j: (0, 0)),
            pl.BlockSpec((D_MODEL, PROJ_TN), lambda i, j: (0, j)),
        ],
        out_specs=pl.BlockSpec((tm, PROJ_TN), lambda i, j: (i, j)),
        out_shape=jax.ShapeDtypeStruct((m, PROJ_COLS), F32),
        scratch_shapes=[pltpu.VMEM((tm, D_MODEL), BF16)],
        compiler_params=_cparams(("parallel", "arbitrary")),
        name="proj",
    )(x, norm_w, w)


def _outproj_kernel(x_ref, ya_ref, yb_ref, wa_ref, wb_ref, nw_ref, x1_ref, h2_ref):
    x1 = (x_ref[...]
          + jnp.dot(ya_ref[...], wa_ref[...], preferred_element_type=F32)
          + jnp.dot(yb_ref[...], wb_ref[...], preferred_element_type=F32))
    x1_ref[...] = x1
    ms = jnp.mean(x1 * x1, axis=-1, keepdims=True)
    h2_ref[...] = (x1 * lax.rsqrt(ms + EPS) * nw_ref[...]).astype(BF16)


def _outproj(x, ya, yb, wa, wb, norm_w, tm):
    m = x.shape[0]
    row = lambda i: (i, 0)
    fixed = lambda i: (0, 0)
    return pl.pallas_call(
        _outproj_kernel,
        grid=(m // tm,),
        in_specs=[
            pl.BlockSpec((tm, D_MODEL), row),
            pl.BlockSpec((tm, A_WIDTH), row),
            pl.BlockSpec((tm, B_WIDTH), row),
            pl.BlockSpec((A_WIDTH, D_MODEL), fixed),
            pl.BlockSpec((B_WIDTH, D_MODEL), fixed),
            pl.BlockSpec((1, D_MODEL), fixed),
        ],
        out_specs=[pl.BlockSpec((tm, D_MODEL), row), pl.BlockSpec((tm, D_MODEL), row)],
        out_shape=[jax.ShapeDtypeStruct((m, D_MODEL), F32),
                   jax.ShapeDtypeStruct((m, D_MODEL), BF16)],
        compiler_params=_cparams(("parallel",)),
        name="outproj",
    )(x, ya, yb, wa, wb, norm_w)


def _ffn1_kernel(h_ref, wg_ref, wu_ref, o_ref):
    h = h_ref[...]
    g = jnp.dot(h, wg_ref[...], preferred_element_type=F32)
    u = jnp.dot(h, wu_ref[...], preferred_element_type=F32)
    o_ref[...] = (g * jax.nn.sigmoid(g) * u).astype(BF16)


def _ffn1(h2, wg, wu, tm, tn):
    m = h2.shape[0]
    return pl.pallas_call(
        _ffn1_kernel,
        grid=(m // tm, D_FF // tn),
        in_specs=[
            pl.BlockSpec((tm, D_MODEL), lambda i, j: (i, 0)),
            pl.BlockSpec((D_MODEL, tn), lambda i, j: (0, j)),
            pl.BlockSpec((D_MODEL, tn), lambda i, j: (0, j)),
        ],
        out_specs=pl.BlockSpec((tm, tn), lambda i, j: (i, j)),
        out_shape=jax.ShapeDtypeStruct((m, D_FF), BF16),
        compiler_params=_cparams(("parallel", "parallel")),
        name="ffn1",
    )(h2, wg, wu)


def _ffn2_kernel(g_ref, wd_ref, x1_ref, nw_ref, o_ref, acc_ref):
    k = pl.program_id(1)

    @pl.when(k == 0)
    def _():
        acc_ref[...] = x1_ref[...]

    acc_ref[...] += jnp.dot(g_ref[...], wd_ref[...], preferred_element_type=F32)

    @pl.when(k == pl.num_programs(1) - 1)
    def _():
        x2 = acc_ref[...]
        ms = jnp.mean(x2 * x2, axis=-1, keepdims=True)
        o_ref[...] = x2 * lax.rsqrt(ms + EPS) * nw_ref[...]


def _ffn2(g, wd, x1, norm_w, tm, tk):
    m = g.shape[0]
    return pl.pallas_call(
        _ffn2_kernel,
        grid=(m // tm, D_FF // tk),
        in_specs=[
            pl.BlockSpec((tm, tk), lambda i, k: (i, k)),
            pl.BlockSpec((tk, D_MODEL), lambda i, k: (k, 0)),
            pl.BlockSpec((tm, D_MODEL), lambda i, k: (i, 0)),
            pl.BlockSpec((1, D_MODEL), lambda i, k: (0, 0)),
        ],
        out_specs=pl.BlockSpec((tm, D_MODEL), lambda i, k: (i, 0)),
        out_shape=jax.ShapeDtypeStruct((m, D_MODEL), F32),
        scratch_shapes=[pltpu.VMEM((tm, D_MODEL), F32)],
        compiler_params=_cparams(("parallel", "arbitrary")),
        name="ffn2",
    )(g, wd, x1, norm_w)


def _shift_rows(z, prev_row):
    rolled = pltpu.roll(z, 1, axis=0)
    row = lax.broadcasted_iota(jnp.int32, z.shape, 0)
    return jnp.where(row == 0, prev_row, rolled)


def _rwkv_kernel(zr_ref, zk_ref, zv_ref, zl_ref, sh_rkv_ref, sh_l_ref, s0_ref,
                 mu_rkv_ref, mu_l_ref, w0_ref, a0_ref, wa_up_ref, g_up_ref,
                 kk_ref, ka_ref, rk_ref, lnw_ref, lnb_ref, ones2_ref,
                 ya_ref, osh_rkv_ref, osh_l_ref, os_ref,
                 prev_ref, s_ref, *, chunk):
    C = chunk
    W = A_WIDTH
    HD = A_HEAD_DIM
    c_idx = pl.program_id(1)

    @pl.when(c_idx == 0)
    def _():
        prev_ref[:, 0:3 * W] = sh_rkv_ref[0]
        prev_ref[:, 3 * W:3 * W + LR_PAD] = sh_l_ref[0]
        s_ref[...] = s0_ref[0]

    PR = range(A_PAIRS)
    cols = lambda t, p: t[:, p * LANES:(p + 1) * LANES]

    def split2(x):
        hi = x.astype(BF16)
        return hi, (x - hi.astype(F32)).astype(BF16)

    def hsum(x):
        hi, lo = split2(jnp.concatenate([cols(x, p) for p in PR], axis=0))
        s = jnp.dot(jnp.concatenate([hi, lo], axis=1), ones2_ref[...], preferred_element_type=F32)
        return jnp.concatenate([s[p * C:(p + 1) * C] for p in PR], axis=1)

    def mixed(z_ref, lo, width, mu):
        z = z_ref[...]
        zp = _shift_rows(z, prev_ref[:, lo:lo + width])
        prev_ref[:, lo:lo + width] = z[C - 1:C, :]
        return z + mu * (zp - z)

    r = mixed(zr_ref, 0, W, mu_rkv_ref[:, 0:W])
    k = mixed(zk_ref, W, W, mu_rkv_ref[:, W:2 * W])
    v = mixed(zv_ref, 2 * W, W, mu_rkv_ref[:, 2 * W:3 * W])
    zl = mixed(zl_ref, 3 * W, LR_PAD, mu_l_ref[...])
    osh_rkv_ref[0] = prev_ref[:, 0:3 * W]
    osh_l_ref[0] = prev_ref[:, 3 * W:3 * W + LR_PAD]

    l1 = zl[:, 0:LANES]
    lane = lax.broadcasted_iota(jnp.int32, l1.shape, 1)
    l1 = jnp.where(lane < DECAY_RANK, jnp.tanh(l1), l1)
    wa = jnp.dot(l1.astype(BF16), wa_up_ref[...], preferred_element_type=F32)
    gl = jax.nn.sigmoid(zl[:, LANES:3 * LANES])
    g = jnp.dot(gl.astype(BF16), g_up_ref[...], preferred_element_type=F32)

    wlog = -jax.nn.softplus(-(w0_ref[...] + wa[:, 0:W])) - 0.5
    lw = -jnp.exp(wlog)
    a = jax.nn.sigmoid(a0_ref[...] + wa[:, W:2 * W])
    kk = k * kk_ref[...]
    kk = kk * lax.rsqrt(jnp.maximum(hsum(kk * kk), 1e-24))
    k2 = k * (1.0 + (a - 1.0) * ka_ref[...])
    ka = kk * a

    ti2 = lax.broadcasted_iota(jnp.int32, (C, 2 * C), 0)
    sj2 = lax.broadcasted_iota(jnp.int32, (C, 2 * C), 1)
    sj2 = jnp.where(sj2 >= C, sj2 - C, sj2)
    incl2 = sj2 <= ti2
    strict2 = sj2 < ti2
    eye2 = jnp.where(sj2 == ti2, 1.0, 0.0)

    lw_hi, lw_lo = split2(lw)
    cum = jnp.dot(jnp.where(incl2, 1.0, 0.0).astype(BF16), jnp.concatenate([lw_hi, lw_lo], axis=0),
                  preferred_element_type=F32)
    cum_last = cum[C - 1:C, :]
    e_neg = jnp.exp(-cum)
    e_end = jnp.exp(cum_last - cum)
    at_b = (kk * jnp.exp(cum - lw)).astype(BF16)
    rt_b = (r * jnp.exp(cum)).astype(BF16)
    kh_b = (k2 * e_neg).astype(BF16)
    bh_b = (ka * e_neg).astype(BF16)
    ke_b = (k2 * e_end).astype(BF16)
    be_b = (-ka * e_end).astype(BF16)
    v_b = v.astype(BF16)
    p_last = jnp.exp(cum_last)

    lane_c = lax.broadcasted_iota(jnp.int32, (C, LANES), 1)
    lane_tt = lax.broadcasted_iota(jnp.int32, (C, 2 * C), 1)
    bd_i = lax.broadcasted_iota(jnp.int32, (LANES, LANES), 0) // HD
    bd_j = lax.broadcasted_iota(jnp.int32, (LANES, LANES), 1) // HD
    bd_mask = bd_i == bd_j
    n_sq = int(math.log2(C)) - 1
    NT = (((1,), (1,)), ((), ()))
    TN = (((0,), (0,)), ((), ()))

    def bd(x, lane, split):
        zero = jnp.zeros_like(x)
        return jnp.concatenate([jnp.where(lane < split, x, zero), jnp.where(lane < split, zero, x)], axis=0)

    bd_ch = lambda x: bd(x, lane_c, HD)
    bd_tt = lambda x: bd(x, lane_tt, C)
    dot = lambda x, y: jnp.dot(x, y, preferred_element_type=F32)

    s_old = [s_ref[p] for p in PR]
    lhs = [jnp.concatenate([cols(at_b, p), cols(rt_b, p)], axis=0) for p in PR]
    rhs = [jnp.concatenate([bd_ch(cols(kh_b, p)), bd_ch(cols(bh_b, p))], axis=0) for p in PR]
    m1 = [lax.dot_general(lhs[p], rhs[p], NT, preferred_element_type=F32) for p in PR]
    m2 = [lax.dot_general(lhs[p], s_old[p].astype(BF16), NT, preferred_element_type=F32) for p in PR]
    a_k = [jnp.where(strict2, m1[p][0:C, 0:2 * C], 0.0).astype(BF16) for p in PR]
    a_b = [jnp.where(strict2, m1[p][0:C, 2 * C:4 * C], 0.0) for p in PR]
    b_kb = [jnp.concatenate([jnp.where(incl2, m1[p][C:2 * C, 0:2 * C], 0.0),
                             jnp.where(incl2, -m1[p][C:2 * C, 2 * C:4 * C], 0.0)], axis=1).astype(BF16)
            for p in PR]
    bd_v = [bd_ch(cols(v_b, p)) for p in PR]
    rhs_u = [m2[p][0:C] + dot(a_k[p], bd_v[p]) for p in PR]
    t_inv = [eye2 - a_b[p] for p in PR]
    pw_b = [a_b[p].astype(BF16) for p in PR]
    pw_bd = [bd_tt(pw_b[p]) for p in PR]
    for _ in range(n_sq):
        pw_b = [dot(pw_b[p], pw_bd[p]).astype(BF16) for p in PR]
        pw_bd = [bd_tt(pw_b[p]) for p in PR]
        t_inv = [t_inv[p] + dot(t_inv[p].astype(BF16), pw_bd[p]) for p in PR]
    u_b = [dot(t_inv[p].astype(BF16), bd_ch(rhs_u[p].astype(BF16))).astype(BF16) for p in PR]
    ys = [m2[p][C:2 * C] + dot(b_kb[p], jnp.concatenate([bd_v[p], bd_ch(u_b[p])], axis=0)) for p in PR]
    upd = [lax.dot_general(jnp.concatenate([cols(v_b, p), u_b[p]], axis=0),
                           jnp.concatenate([cols(ke_b, p), cols(be_b, p)], axis=0),
                           TN, preferred_element_type=F32) for p in PR]
    for p in PR:
        s_ref[p] = s_old[p] * cols(p_last, p) + jnp.where(bd_mask, upd[p], 0.0)

    y = jnp.concatenate(ys, axis=1)
    inv_n = 1.0 / A_HEAD_DIM
    mean = hsum(y) * inv_n
    d = y - mean
    var = hsum(d * d) * inv_n
    y = d * lax.rsqrt(var + LNX_EPS) * lnw_ref[...] + lnb_ref[...]
    y = y + hsum(r * k2 * rk_ref[...]) * v
    ya_ref[...] = (y * g).astype(BF16)

    @pl.when(c_idx == pl.num_programs(1) - 1)
    def _():
        os_ref[0] = s_ref[...]


def _rwkv(proj, row_off, nb, nc, chunk, sh_rkv, sh_l, s0, shared_state, wts):
    C = chunk
    rb = row_off // C
    W = A_WIDTH
    zspec = lambda width, col: pl.BlockSpec((C, width), lambda b, c: (rb + b * nc + c, col // width))
    st = (lambda b: 0) if shared_state else (lambda b: b)
    fixed2 = lambda b, c: (0, 0)
    full = lambda arr: pl.BlockSpec(arr.shape, fixed2)
    in_specs = [
        zspec(W, COL_R), zspec(W, COL_K), zspec(W, COL_V), zspec(LR_PAD, COL_LR),
        pl.BlockSpec((1, 1, 3 * W), lambda b, c: (st(b), 0, 0)),
        pl.BlockSpec((1, 1, LR_PAD), lambda b, c: (st(b), 0, 0)),
        pl.BlockSpec((1, A_PAIRS, LANES, LANES), lambda b, c: (st(b), 0, 0, 0)),
    ] + [full(w) for w in wts]
    out_specs = [
        pl.BlockSpec((C, W), lambda b, c: (b * nc + c, 0)),
        pl.BlockSpec((1, 1, 3 * W), lambda b, c: (b, 0, 0)),
        pl.BlockSpec((1, 1, LR_PAD), lambda b, c: (b, 0, 0)),
        pl.BlockSpec((1, A_PAIRS, LANES, LANES), lambda b, c: (b, 0, 0, 0)),
    ]
    out_shape = [
        jax.ShapeDtypeStruct((nb * nc * C, W), BF16),
        jax.ShapeDtypeStruct((nb, 1, 3 * W), F32),
        jax.ShapeDtypeStruct((nb, 1, LR_PAD), F32),
        jax.ShapeDtypeStruct((nb, A_PAIRS, LANES, LANES), F32),
    ]
    return pl.pallas_call(
        functools.partial(_rwkv_kernel, chunk=C),
        grid=(nb, nc),
        in_specs=in_specs,
        out_specs=out_specs,
        out_shape=out_shape,
        scratch_shapes=[
            pltpu.VMEM((1, 3 * W + LR_PAD), F32),
            pltpu.VMEM((A_PAIRS, LANES, LANES), F32),
        ],
        compiler_params=_cparams(("parallel", "arbitrary")),
        name="rwkv",
    )(proj, proj, proj, proj, sh_rkv, sh_l, s0, *wts)


def _mamba_kernel(z_ref, x_ref, bc_ref, dt_ref, cx0_ref, cbc0_ref, s0_ref,
                  cwx_ref, cwbc_ref, cbx_ref, cbbc_ref, dtb_ref, alog_ref, dsk_ref, nw_ref, e_ref,
                  yb_ref, ocx_ref, ocbc_ref, os_ref,
                  cx_ref, cbc_ref, s_ref, y_ref, *, chunk):
    C = chunk
    c_idx = pl.program_id(1)
    GW = B_WIDTH // B_GROUPS
    HG = B_HEADS // B_GROUPS

    @pl.when(c_idx == 0)
    def _():
        cx_ref[...] = cx0_ref[0]
        cbc_ref[...] = cbc0_ref[0]
        s_ref[...] = s0_ref[0]

    def causal_conv(raw_ref, carry_ref, taps_ref, bias_ref):
        raw = raw_ref[...]
        ext = jnp.concatenate([carry_ref[...], raw], axis=0)
        acc = bias_ref[...]
        for j in range(CONV_W):
            shifted = ext if j == CONV_W - 1 else pltpu.roll(ext, CONV_W - 1 - j, axis=0)
            acc = acc + taps_ref[j:j + 1, :] * shifted[CONV_ROWS:CONV_ROWS + C, :]
        carry_ref[...] = raw[C - CONV_ROWS:C, :]
        return acc * jax.nn.sigmoid(acc)

    xs = causal_conv(x_ref, cx_ref, cwx_ref, cbx_ref)
    bcm = causal_conv(bc_ref, cbc_ref, cwbc_ref, cbbc_ref)
    ocx_ref[0] = cx_ref[...]
    ocbc_ref[0] = cbc_ref[...]

    dt = jax.nn.softplus(dt_ref[...] + dtb_ref[...])
    d_a = dt * (-jnp.exp(alog_ref[...]))
    ti = lax.broadcasted_iota(jnp.int32, (C, C), 0)
    si = lax.broadcasted_iota(jnp.int32, (C, C), 1)
    causal = si <= ti
    a_cs = jnp.dot(causal.astype(F32), d_a, precision=HI, preferred_element_type=F32)
    expand = lambda t: jnp.dot(t, e_ref[...], precision=HI, preferred_element_type=F32)
    dt_e = expand(dt)
    acs_e = expand(a_cs)
    alast_e = acs_e[C - 1:C, :]
    acs_t = a_cs.T

    xdt = xs * dt_e
    xdt_b = xdt.astype(BF16)
    xend_b = (xdt * jnp.exp(alast_e - acs_e)).astype(BF16)
    s_decay = jnp.exp(alast_e)
    in_decay = jnp.exp(acs_e)

    for g in range(B_GROUPS):
        bm = bcm[:, g * D_STATE:(g + 1) * D_STATE].astype(BF16)
        cm = bcm[:, (B_GROUPS + g) * D_STATE:(B_GROUPS + g + 1) * D_STATE].astype(BF16)
        cb = lax.dot_general(cm, bm, (((1,), (1,)), ((), ())), preferred_element_type=F32)
        gsl = slice(g * GW, (g + 1) * GW)
        s_old = s_ref[:, gsl]
        y_state = jnp.dot(cm, s_old.astype(BF16), preferred_element_type=F32) * in_decay[:, gsl]
        for hh in range(HG):
            h = g * HG + hh
            hsl = slice(h * B_HEAD_DIM, (h + 1) * B_HEAD_DIM)
            seg = acs_e[:, h * B_HEAD_DIM:h * B_HEAD_DIM + C] - acs_t[h:h + 1, :]
            m = jnp.where(causal, jnp.exp(seg), 0.0) * cb
            y_ref[:, hsl] = jnp.dot(m.astype(BF16), xdt_b[:, hsl], preferred_element_type=F32)
        y_ref[:, gsl] = y_ref[:, gsl] + y_state
        s_ref[:, gsl] = s_old * s_decay[:, gsl] + lax.dot_general(
            bm, xend_b[:, gsl], (((0,), (0,)), ((), ())), preferred_element_type=F32)

    z = z_ref[...]
    y = (y_ref[...] + xs * dsk_ref[...]) * (z * jax.nn.sigmoid(z))
    outs = []
    for g in range(B_GROUPS):
        yg = y[:, g * GW:(g + 1) * GW]
        ms = jnp.mean(yg * yg, axis=-1, keepdims=True)
        outs.append(yg * lax.rsqrt(ms + EPS))
    yb_ref[...] = (jnp.concatenate(outs, axis=1) * nw_ref[...]).astype(BF16)

    @pl.when(c_idx == pl.num_programs(1) - 1)
    def _():
        os_ref[0] = s_ref[...]


def _mamba(proj, row_off, nb, nc, chunk, cx0, cbc0, s0, shared_state, wts):
    C = chunk
    assert C >= CONV_ROWS and C <= B_HEAD_DIM
    rb = row_off // C
    W = B_WIDTH
    zspec = lambda width, col: pl.BlockSpec((C, width), lambda b, c: (rb + b * nc + c, col // width))
    st = (lambda b: 0) if shared_state else (lambda b: b)
    full = lambda arr: pl.BlockSpec(arr.shape, lambda b, c: (0, 0))
    in_specs = [
        zspec(W, COL_Z), zspec(W, COL_X), zspec(BC_WIDTH, COL_BC), zspec(DT_PAD, COL_DT),
        pl.BlockSpec((1, CONV_ROWS, W), lambda b, c: (st(b), 0, 0)),
        pl.BlockSpec((1, CONV_ROWS, BC_WIDTH), lambda b, c: (st(b), 0, 0)),
        pl.BlockSpec((1, D_STATE, W), lambda b, c: (st(b), 0, 0)),
    ] + [full(w) for w in wts]
    out_specs = [
        pl.BlockSpec((C, W), lambda b, c: (b * nc + c, 0)),
        pl.BlockSpec((1, CONV_ROWS, W), lambda b, c: (b, 0, 0)),
        pl.BlockSpec((1, CONV_ROWS, BC_WIDTH), lambda b, c: (b, 0, 0)),
        pl.BlockSpec((1, D_STATE, W), lambda b, c: (b, 0, 0)),
    ]
    out_shape = [
        jax.ShapeDtypeStruct((nb * nc * C, W), BF16),
        jax.ShapeDtypeStruct((nb, CONV_ROWS, W), F32),
        jax.ShapeDtypeStruct((nb, CONV_ROWS, BC_WIDTH), F32),
        jax.ShapeDtypeStruct((nb, D_STATE, W), F32),
    ]
    return pl.pallas_call(
        functools.partial(_mamba_kernel, chunk=C),
        grid=(nb, nc),
        in_specs=in_specs,
        out_specs=out_specs,
        out_shape=out_shape,
        scratch_shapes=[
            pltpu.VMEM((CONV_ROWS, W), F32),
            pltpu.VMEM((CONV_ROWS, BC_WIDTH), F32),
            pltpu.VMEM((D_STATE, W), F32),
            pltpu.VMEM((C, W), F32),
        ],
        compiler_params=_cparams(("parallel", "arbitrary")),
        name="mamba",
    )(proj, proj, proj, proj, cx0, cbc0, s0, *wts)


def _pad_cols(a, width):
    return jnp.pad(a, [(0, 0)] * (a.ndim - 1) + [(0, width - a.shape[-1])])


def _head_indicator(n_heads, head_dim, rows):
    col_head = jnp.arange(n_heads * head_dim, dtype=jnp.int32) // head_dim
    return (jnp.arange(rows, dtype=jnp.int32)[:, None] == col_head[None, :]).astype(F32)


def _wkv_to_pairs(s):
    nb = s.shape[0]
    s = s.reshape(nb, A_PAIRS, 2, A_HEAD_DIM, A_HEAD_DIM)
    zero = jnp.zeros_like(s[:, :, 0])
    top = jnp.concatenate([s[:, :, 0], zero], axis=-1)
    bottom = jnp.concatenate([zero, s[:, :, 1]], axis=-1)
    return jnp.concatenate([top, bottom], axis=-2)


def _wkv_from_pairs(s):
    nb = s.shape[0]
    s = s.reshape(nb, A_PAIRS, 2, A_HEAD_DIM, 2, A_HEAD_DIM)
    s = jnp.stack([s[:, :, 0, :, 0, :], s[:, :, 1, :, 1, :]], axis=2)
    return s.reshape(nb, A_HEADS, A_HEAD_DIM, A_HEAD_DIM)


def kernel(x_prompt, x_sample, state_rwkv_shift, state_rwkv_wkv, state_ssm_conv, state_ssm, meta_tokens, norm_mix_w, w_in, rwkv_mu, rwkv_w0, rwkv_w_up, rwkv_a0, rwkv_a_up, rwkv_g_up, rwkv_k_k, rwkv_k_a, rwkv_r_k, rwkv_lnx_w, rwkv_lnx_b, ssm_conv_w, ssm_conv_b, ssm_dt_bias, ssm_A_log, ssm_D, ssm_norm_w, w_out, norm_ffn_w, ffn_w_gate, ffn_w_up, ffn_w_down, norm_final_w):
    depth = w_in.shape[0]
    assert depth == 1, "single-layer step"
    bp, lp, d = x_prompt.shape
    bs, ls, _ = x_sample.shape
    n_meta = meta_tokens.shape[0]
    assert d == D_MODEL and lp % CHUNK == 0 and ls <= CHUNK and n_meta <= CHUNK
    assert w_in.shape[-1] == A_COLS + 2 * B_WIDTH + BC_WIDTH + B_HEADS
    W = A_WIDTH

    wi = w_in[0]
    w_perm = jnp.concatenate([
        wi[:, 0:3 * W],
        wi[:, A_COLS:A_COLS + 2 * B_WIDTH + BC_WIDTH],
        _pad_cols(wi[:, 3 * W:A_COLS], LR_PAD),
        _pad_cols(wi[:, A_COLS + 2 * B_WIDTH + BC_WIDTH:], DT_PAD),
    ], axis=1).astype(BF16)
    nmw = norm_mix_w.reshape(1, D_MODEL)
    mu = rwkv_mu[0]
    mu_rkv = mu[None, 0:3 * W]
    mu_l = _pad_cols(mu[None, 3 * W:], LR_PAD)
    wa_up = jnp.zeros((LANES, 2 * W), F32)
    wa_up = wa_up.at[0:DECAY_RANK, 0:W].set(rwkv_w_up[0])
    wa_up = wa_up.at[DECAY_RANK:DECAY_RANK + AAA_RANK, W:].set(rwkv_a_up[0]).astype(BF16)
    g_up = jnp.pad(rwkv_g_up[0], ((0, 2 * LANES - GATE_RANK), (0, 0))).astype(BF16)
    lane_head = jnp.arange(LANES, dtype=jnp.int32) // A_HEAD_DIM
    ones_bd = (lane_head[:, None] == lane_head[None, :]).astype(BF16)
    ones2 = jnp.concatenate([ones_bd, ones_bd], axis=0)
    rwkv_wts = (mu_rkv, mu_l, rwkv_w0.reshape(1, W), rwkv_a0.reshape(1, W), wa_up, g_up,
                rwkv_k_k.reshape(1, W), rwkv_k_a.reshape(1, W), rwkv_r_k.reshape(1, W),
                rwkv_lnx_w.reshape(1, W), rwkv_lnx_b.reshape(1, W), ones2)
    taps = ssm_conv_w[0].T
    cbias = ssm_conv_b.reshape(1, -1)
    e_b = _head_indicator(B_HEADS, B_HEAD_DIM, LANES)
    mamba_wts = (taps[:, :B_WIDTH], taps[:, B_WIDTH:], cbias[:, :B_WIDTH], cbias[:, B_WIDTH:],
                 _pad_cols(ssm_dt_bias.reshape(1, B_HEADS), DT_PAD),
                 _pad_cols(ssm_A_log.reshape(1, B_HEADS), DT_PAD),
                 jnp.repeat(ssm_D.reshape(1, B_HEADS), B_HEAD_DIM, axis=1),
                 ssm_norm_w.reshape(1, B_WIDTH), e_b)
    wo = w_out[0].astype(BF16)
    wo_a, wo_b = wo[:W], wo[W:]
    nfw = norm_ffn_w.reshape(1, D_MODEL)
    wg = ffn_w_gate[0].astype(BF16)
    wu = ffn_w_up[0].astype(BF16)
    wd = ffn_w_down[0].astype(BF16)
    nlw = norm_final_w.reshape(1, D_MODEL)

    tm_small = 384
    n_small = bs * ls + n_meta
    m_small = -(-n_small // tm_small) * tm_small
    x_small = jnp.concatenate([x_sample.reshape(bs * ls, d), meta_tokens,
                               jnp.zeros((m_small - n_small, d), F32)], axis=0)
    x_big = x_prompt.reshape(bp * lp, d)
    tm_big = math.gcd(bp * lp, 512)

    proj_small = _proj(x_small, nmw, w_perm, tm_small)
    proj_big = _proj(x_big, nmw, w_perm, tm_big)

    zeros = lambda *s: jnp.zeros(s, F32)
    sh = state_rwkv_shift[0]
    smp_rwkv = (sh[:, None, 0:3 * W], _pad_cols(sh[:, None, 3 * W:], LR_PAD),
                _wkv_to_pairs(state_rwkv_wkv[0]))
    conv0 = jnp.pad(state_ssm_conv[0], ((0, 0), (CONV_ROWS - (CONV_W - 1), 0), (0, 0)))
    ssm0 = jnp.transpose(state_ssm[0], (0, 3, 1, 2)).reshape(bs, D_STATE, B_WIDTH)
    smp_mamba = (conv0[..., :B_WIDTH], conv0[..., B_WIDTH:], ssm0)
    meta_rwkv0 = (zeros(1, 1, 3 * W), zeros(1, 1, LR_PAD), zeros(1, A_PAIRS, LANES, LANES))
    meta_mamba0 = (zeros(1, CONV_ROWS, B_WIDTH), zeros(1, CONV_ROWS, BC_WIDTH), zeros(1, D_STATE, B_WIDTH))

    ya_m, *meta_rwkv = _rwkv(proj_small, bs * ls, 1, 1, n_meta, *meta_rwkv0, False, rwkv_wts)
    yb_m, *meta_mamba = _mamba(proj_small, bs * ls, 1, 1, n_meta, *meta_mamba0, False, mamba_wts)
    ya_s, *s_rwkv = _rwkv(proj_small, 0, bs, 1, ls, *smp_rwkv, False, rwkv_wts)
    yb_s, *s_mamba = _mamba(proj_small, 0, bs, 1, ls, *smp_mamba, False, mamba_wts)
    ya_p, *p_rwkv = _rwkv(proj_big, 0, bp, lp // CHUNK, CHUNK, *meta_rwkv, True, rwkv_wts)
    yb_p, *p_mamba = _mamba(proj_big, 0, bp, lp // CHUNK, CHUNK, *meta_mamba, True, mamba_wts)

    def tail(ya, yb, x, tm):
        x1, h2 = _outproj(x, ya, yb, wo_a, wo_b, nfw, tm)
        g = _ffn1(h2, wg, wu, tm, 512)
        return _ffn2(g, wd, x1, nlw, tm, D_FF // 4)

    pad_rows = lambda *parts: jnp.concatenate(
        list(parts) + [jnp.zeros((m_small - n_small, parts[0].shape[1]), parts[0].dtype)], axis=0)
    y_small = tail(pad_rows(ya_s, ya_m), pad_rows(yb_s, yb_m), x_small, tm_small)
    y_big = tail(ya_p, yb_p, x_big, tm_big)

    def states_out(rw, mb, nb):
        sh_rkv, sh_l, wkv = rw
        cx, cbc, ssm = mb
        shift = jnp.concatenate([sh_rkv[:, 0], sh_l[:, 0, :LOWRANK]], axis=-1)[None]
        wkv = _wkv_from_pairs(wkv)[None]
        conv = jnp.concatenate([cx, cbc], axis=-1)[None, :, CONV_ROWS - (CONV_W - 1):]
        ssm = jnp.transpose(ssm.reshape(nb, D_STATE, B_HEADS, B_HEAD_DIM), (0, 2, 3, 1))[None]
        return shift, wkv, conv, ssm

    y_prompt = y_big.reshape(bp, lp, d)
    y_sample = y_small[:bs * ls].reshape(bs, ls, d)
    return (y_prompt, y_sample) + states_out(p_rwkv, p_mamba, bp) + states_out(s_rwkv, s_mamba, bs)
```

```python
import functools
import math

import jax
import jax.numpy as jnp
from jax import lax
from jax.experimental import pallas as pl
from jax.experimental.pallas import tpu as pltpu

F32 = jnp.float32
BF16 = jnp.bfloat16

D_MODEL = 2048
CHUNK = 64
EPS = 1e-6
LNX_EPS = 64e-5
A_WIDTH = 1024
A_HEADS = 16
A_HEAD_DIM = 64
A_PAIRS = A_HEADS // 2
DECAY_RANK = 64
AAA_RANK = 64
GATE_RANK = 160
B_WIDTH = 1024
B_HEADS = 16
B_HEAD_DIM = 64
B_PAIRS = B_HEADS // 2
B_GROUPS = 2
D_STATE = 128
CONV_W = 4
BC_WIDTH = 2 * B_GROUPS * D_STATE
A_COLS = 3 * A_WIDTH + DECAY_RANK + AAA_RANK + GATE_RANK
LOWRANK = DECAY_RANK + AAA_RANK + GATE_RANK
D_FF = 5632

LANES = 128
SUBLANES = 8
VMEM_LIMIT = 56 * 1024 * 1024

LR_PAD = 512
DT_PAD = 128
COL_R, COL_K, COL_V, COL_Z, COL_X = 0, 1024, 2048, 3072, 4096
COL_BC = 5120
COL_LR = 5632
COL_DT = 6144
PROJ_COLS = COL_DT + DT_PAD
PROJ_TN = 896
CONV_ROWS = SUBLANES

NT = (((1,), (1,)), ((), ()))
TN = (((0,), (0,)), ((), ()))


def _cparams(sem):
    return pltpu.CompilerParams(dimension_semantics=sem, vmem_limit_bytes=VMEM_LIMIT)


def _dot(x, y):
    return jnp.dot(x, y, preferred_element_type=F32)


def _block_diag_rhs(x, lane, split):
    zero = jnp.zeros_like(x)
    return jnp.concatenate([jnp.where(lane < split, x, zero), jnp.where(lane < split, zero, x)], axis=0)


def _split3(x):
    hi = x.astype(BF16)
    r1 = x - hi.astype(F32)
    mid = r1.astype(BF16)
    return hi, mid, (r1 - mid.astype(F32)).astype(BF16)


def _proj_kernel(x_ref, nw_ref, w_ref, o_ref, h_ref):
    @pl.when(pl.program_id(1) == 0)
    def _():
        x = x_ref[...]
        ms = jnp.mean(x * x, axis=-1, keepdims=True)
        h_ref[...] = (x * lax.rsqrt(ms + EPS) * nw_ref[...]).astype(BF16)

    o_ref[...] = _dot(h_ref[...], w_ref[...])


def _proj(x, norm_w, w, tm):
    m = x.shape[0]
    return pl.pallas_call(
        _proj_kernel,
        grid=(m // tm, PROJ_COLS // PROJ_TN),
        in_specs=[
            pl.BlockSpec((tm, D_MODEL), lambda i, j: (i, 0)),
            pl.BlockSpec((1, D_MODEL), lambda i, j: (0, 0)),
            pl.BlockSpec((D_MODEL, PROJ_TN), lambda i, j: (0, j)),
        ],
        out_specs=pl.BlockSpec((tm, PROJ_TN), lambda i, j: (i, j)),
        out_shape=jax.ShapeDtypeStruct((m, PROJ_COLS), F32),
        scratch_shapes=[pltpu.VMEM((tm, D_MODEL), BF16)],
        compiler_params=_cparams(("parallel", "arbitrary")),
        name="proj",
    )(x, norm_w, w)


def _outproj_kernel(x_ref, ya_ref, yb_ref, wa_ref, wb_ref, nw_ref, x1_ref, h2_ref):
    x1 = x_ref[...] + _dot(ya_ref[...], wa_ref[...]) + _dot(yb_ref[...], wb_ref[...])
    x1_ref[...] = x1
    ms = jnp.mean(x1 * x1, axis=-1, keepdims=True)
    h2_ref[...] = (x1 * lax.rsqrt(ms + EPS) * nw_ref[...]).astype(BF16)


def _outproj(x, ya, yb, wa, wb, norm_w, tm):
    m = x.shape[0]
    row = lambda i: (i, 0)
    fixed = lambda i: (0, 0)
    return pl.pallas_call(
        _outproj_kernel,
        grid=(m // tm,),
        in_specs=[
            pl.BlockSpec((tm, D_MODEL), row),
            pl.BlockSpec((tm, A_WIDTH), row),
            pl.BlockSpec((tm, B_WIDTH), row),
            pl.BlockSpec((A_WIDTH, D_MODEL), fixed),
            pl.BlockSpec((B_WIDTH, D_MODEL), fixed),
            pl.BlockSpec((1, D_MODEL), fixed),
        ],
        out_specs=[pl.BlockSpec((tm, D_MODEL), row), pl.BlockSpec((tm, D_MODEL), row)],
        out_shape=[jax.ShapeDtypeStruct((m, D_MODEL), F32),
                   jax.ShapeDtypeStruct((m, D_MODEL), BF16)],
        compiler_params=_cparams(("parallel",)),
        name="outproj",
    )(x, ya, yb, wa, wb, norm_w)


def _ffn1_kernel(h_ref, wg_ref, wu_ref, o_ref):
    h = h_ref[...]
    g = _dot(h, wg_ref[...])
    u = _dot(h, wu_ref[...])
    o_ref[...] = (g * jax.nn.sigmoid(g) * u).astype(BF16)


def _ffn1(h2, wg, wu, tm, tn):
    m = h2.shape[0]
    return pl.pallas_call(
        _ffn1_kernel,
        grid=(m // tm, D_FF // tn),
        in_specs=[
            pl.BlockSpec((tm, D_MODEL), lambda i, j: (i, 0)),
            pl.BlockSpec((D_MODEL, tn), lambda i, j: (0, j)),
            pl.BlockSpec((D_MODEL, tn), lambda i, j: (0, j)),
        ],
        out_specs=pl.BlockSpec((tm, tn), lambda i, j: (i, j)),
        out_shape=jax.ShapeDtypeStruct((m, D_FF), BF16),
        compiler_params=_cparams(("parallel", "parallel")),
        name="ffn1",
    )(h2, wg, wu)


def _ffn2_kernel(g_ref, wd_ref, x1_ref, nw_ref, o_ref):
    k = pl.program_id(1)

    @pl.when(k == 0)
    def _():
        o_ref[...] = x1_ref[...] + _dot(g_ref[...], wd_ref[...])

    @pl.when(jnp.logical_and(k > 0, k < pl.num_programs(1) - 1))
    def _():
        o_ref[...] += _dot(g_ref[...], wd_ref[...])

    @pl.when(k == pl.num_programs(1) - 1)
    def _():
        x2 = o_ref[...] + _dot(g_ref[...], wd_ref[...])
        ms = jnp.mean(x2 * x2, axis=-1, keepdims=True)
        o_ref[...] = x2 * lax.rsqrt(ms + EPS) * nw_ref[...]


def _ffn2(g, wd, x1, norm_w, tm, tk):
    m = g.shape[0]
    assert D_FF // tk >= 2
    return pl.pallas_call(
        _ffn2_kernel,
        grid=(m // tm, D_FF // tk),
        in_specs=[
            pl.BlockSpec((tm, tk), lambda i, k: (i, k)),
            pl.BlockSpec((tk, D_MODEL), lambda i, k: (k, 0)),
            pl.BlockSpec((tm, D_MODEL), lambda i, k: (i, 0)),
            pl.BlockSpec((1, D_MODEL), lambda i, k: (0, 0)),
        ],
        out_specs=pl.BlockSpec((tm, D_MODEL), lambda i, k: (i, 0)),
        out_shape=jax.ShapeDtypeStruct((m, D_MODEL), F32),
        compiler_params=_cparams(("parallel", "arbitrary")),
        name="ffn2",
    )(g, wd, x1, norm_w)


def _shift_rows(z, prev_row):
    rolled = pltpu.roll(z, 1, axis=0)
    row = lax.broadcasted_iota(jnp.int32, z.shape, 0)
    return jnp.where(row == 0, prev_row, rolled)


def _rwkv_kernel(zr_ref, zk_ref, zv_ref, zl_ref, sh_rkv_ref, sh_l_ref, s0_ref,
                 mu_rkv_ref, mu_l_ref, w0_ref, a0_ref, wa_up_ref, g_up_ref,
                 kk_ref, ka_ref, rk_ref, lnw_ref, lnb_ref, ones_ref,
                 ya_ref, osh_rkv_ref, osh_l_ref, os_ref,
                 prev_ref, s_ref, *, chunk):
    C = chunk
    W = A_WIDTH
    HD = A_HEAD_DIM
    c_idx = pl.program_id(1)
    PR = range(A_PAIRS)
    cols = lambda t, p: t[:, p * LANES:(p + 1) * LANES]

    @pl.when(c_idx == 0)
    def _():
        prev_ref[:, 0:3 * W] = sh_rkv_ref[0]
        prev_ref[:, 3 * W:3 * W + LR_PAD] = sh_l_ref[0]
        zero = jnp.zeros((HD, HD), F32)
        for p in PR:
            s_ref[p] = jnp.concatenate(
                [jnp.concatenate([s0_ref[0, 2 * p], zero], axis=1),
                 jnp.concatenate([zero, s0_ref[0, 2 * p + 1]], axis=1)], axis=0)

    def hsum(x):
        xs = jnp.concatenate([cols(x, p) for p in PR], axis=0).astype(BF16)
        s = _dot(xs, ones_ref[...])
        return jnp.concatenate([s[p * C:(p + 1) * C] for p in PR], axis=1)

    def mixed(z_ref, lo, width, mu):
        z = z_ref[...]
        zp = _shift_rows(z, prev_ref[:, lo:lo + width])
        prev_ref[:, lo:lo + width] = z[C - 1:C, :]
        return z + mu * (zp - z)

    r = mixed(zr_ref, 0, W, mu_rkv_ref[:, 0:W])
    k = mixed(zk_ref, W, W, mu_rkv_ref[:, W:2 * W])
    v = mixed(zv_ref, 2 * W, W, mu_rkv_ref[:, 2 * W:3 * W])
    zl = mixed(zl_ref, 3 * W, LR_PAD, mu_l_ref[...])
    osh_rkv_ref[0] = prev_ref[:, 0:3 * W]
    osh_l_ref[0] = prev_ref[:, 3 * W:3 * W + LR_PAD]

    l1 = zl[:, 0:LANES]
    lane = lax.broadcasted_iota(jnp.int32, l1.shape, 1)
    l1 = jnp.where(lane < DECAY_RANK, jnp.tanh(l1), l1)
    wa = _dot(l1.astype(BF16), wa_up_ref[...])
    gl = jax.nn.sigmoid(zl[:, LANES:3 * LANES])
    g = _dot(gl.astype(BF16), g_up_ref[...])

    wlog = -jax.nn.softplus(-(w0_ref[...] + wa[:, 0:W])) - 0.5
    lw = -jnp.exp(wlog)
    a = jax.nn.sigmoid(a0_ref[...] + wa[:, W:2 * W])
    kk = k * kk_ref[...]
    kk = kk * lax.rsqrt(jnp.maximum(hsum(kk * kk), 1e-24))
    k2 = k * (1.0 + (a - 1.0) * ka_ref[...])
    ka = kk * a

    ti2 = lax.broadcasted_iota(jnp.int32, (C, 2 * C), 0)
    lane_tt = lax.broadcasted_iota(jnp.int32, (C, 2 * C), 1)
    sj2 = jnp.where(lane_tt >= C, lane_tt - C, lane_tt)
    incl2 = sj2 <= ti2
    strict2 = sj2 < ti2

    lw_hi = lw.astype(BF16)
    lw_lo = (lw - lw_hi.astype(F32)).astype(BF16)
    cum = _dot(jnp.where(incl2, 1.0, 0.0).astype(BF16), jnp.concatenate([lw_hi, lw_lo], axis=0))
    cum_last = cum[C - 1:C, :]
    e_neg = jnp.exp(-cum)
    e_end = jnp.exp(cum_last - cum)
    at_b = (kk * jnp.exp(cum - lw)).astype(BF16)
    rt_b = (r * jnp.exp(cum)).astype(BF16)
    kh_b = (k2 * e_neg).astype(BF16)
    bh_b = (ka * e_neg).astype(BF16)
    ke_b = (k2 * e_end).astype(BF16)
    be_b = (-ka * e_end).astype(BF16)
    v_b = v.astype(BF16)
    p_last = jnp.exp(cum_last)

    lane_c = lax.broadcasted_iota(jnp.int32, (C, LANES), 1)
    bd_i = lax.broadcasted_iota(jnp.int32, (LANES, LANES), 0) // HD
    bd_j = lax.broadcasted_iota(jnp.int32, (LANES, LANES), 1) // HD
    bd_mask = bd_i == bd_j
    n_sq = int(math.log2(C)) - 1
    bd_ch = lambda x: _block_diag_rhs(x, lane_c, HD)
    bd_tt = lambda x: _block_diag_rhs(x, lane_tt, C)

    s_old = [s_ref[p] for p in PR]
    lhs = [jnp.concatenate([cols(at_b, p), cols(rt_b, p)], axis=0) for p in PR]
    rhs = [jnp.concatenate([bd_ch(cols(kh_b, p)), bd_ch(cols(bh_b, p))], axis=0) for p in PR]
    m1 = [lax.dot_general(lhs[p], rhs[p], NT, preferred_element_type=F32) for p in PR]
    m2 = [lax.dot_general(lhs[p], s_old[p].astype(BF16), NT, preferred_element_type=F32) for p in PR]
    a_k = [jnp.where(strict2, m1[p][0:C, 0:2 * C], 0.0).astype(BF16) for p in PR]
    a_b = [jnp.where(strict2, m1[p][0:C, 2 * C:4 * C], 0.0) for p in PR]
    b_kb = [jnp.concatenate([jnp.where(incl2, m1[p][C:2 * C, 0:2 * C], 0.0),
                             jnp.where(incl2, -m1[p][C:2 * C, 2 * C:4 * C], 0.0)], axis=1).astype(BF16)
            for p in PR]
    bd_v = [bd_ch(cols(v_b, p)) for p in PR]
    u = [m2[p][0:C] + _dot(a_k[p], bd_v[p]) for p in PR]
    pw_b = [a_b[p].astype(BF16) for p in PR]
    u = [u[p] - _dot(pw_b[p], bd_ch(u[p].astype(BF16))) for p in PR]
    for _ in range(n_sq):
        pw_b = [_dot(pw_b[p], bd_tt(pw_b[p])).astype(BF16) for p in PR]
        u = [u[p] + _dot(pw_b[p], bd_ch(u[p].astype(BF16))) for p in PR]
    u_b = [u[p].astype(BF16) for p in PR]
    ys = [m2[p][C:2 * C] + _dot(b_kb[p], jnp.concatenate([bd_v[p], bd_ch(u_b[p])], axis=0)) for p in PR]
    upd = [lax.dot_general(jnp.concatenate([cols(v_b, p), u_b[p]], axis=0),
                           jnp.concatenate([cols(ke_b, p), cols(be_b, p)], axis=0),
                           TN, preferred_element_type=F32) for p in PR]
    for p in PR:
        s_ref[p] = s_old[p] * cols(p_last, p) + jnp.where(bd_mask, upd[p], 0.0)

    y = jnp.concatenate(ys, axis=1)
    inv_n = 1.0 / A_HEAD_DIM
    mean = hsum(y) * inv_n
    d = y - mean
    var = hsum(d * d) * inv_n
    y = d * lax.rsqrt(var + LNX_EPS) * lnw_ref[...] + lnb_ref[...]
    y = y + hsum(r * k2 * rk_ref[...]) * v
    ya_ref[...] = (y * g).astype(BF16)

    @pl.when(c_idx == pl.num_programs(1) - 1)
    def _():
        for p in PR:
            s_pair = s_ref[p]
            os_ref[0, 2 * p] = s_pair[0:HD, 0:HD]
            os_ref[0, 2 * p + 1] = s_pair[HD:2 * HD, HD:2 * HD]


def _rwkv(proj, row_off, nb, nc, chunk, sh_rkv, sh_l, s0, shared_state, wts):
    C = chunk
    rb = row_off // C
    W = A_WIDTH
    zspec = lambda width, col: pl.BlockSpec((C, width), lambda b, c: (rb + b * nc + c, col // width))
    st = (lambda b: 0) if shared_state else (lambda b: b)
    full = lambda arr: pl.BlockSpec(arr.shape, lambda b, c: (0, 0))
    state_block = (1, A_HEADS, A_HEAD_DIM, A_HEAD_DIM)
    in_specs = [
        zspec(W, COL_R), zspec(W, COL_K), zspec(W, COL_V), zspec(LR_PAD, COL_LR),
        pl.BlockSpec((1, 1, 3 * W), lambda b, c: (st(b), 0, 0)),
        pl.BlockSpec((1, 1, LR_PAD), lambda b, c: (st(b), 0, 0)),
        pl.BlockSpec(state_block, lambda b, c: (st(b), 0, 0, 0)),
    ] + [full(w) for w in wts]
    out_specs = [
        pl.BlockSpec((C, W), lambda b, c: (b * nc + c, 0)),
        pl.BlockSpec((1, 1, 3 * W), lambda b, c: (b, 0, 0)),
        pl.BlockSpec((1, 1, LR_PAD), lambda b, c: (b, 0, 0)),
        pl.BlockSpec(state_block, lambda b, c: (b, 0, 0, 0)),
    ]
    out_shape = [
        jax.ShapeDtypeStruct((nb * nc * C, W), BF16),
        jax.ShapeDtypeStruct((nb, 1, 3 * W), F32),
        jax.ShapeDtypeStruct((nb, 1, LR_PAD), F32),
        jax.ShapeDtypeStruct((nb,) + state_block[1:], F32),
    ]
    return pl.pallas_call(
        functools.partial(_rwkv_kernel, chunk=C),
        grid=(nb, nc),
        in_specs=in_specs,
        out_specs=out_specs,
        out_shape=out_shape,
        scratch_shapes=[
            pltpu.VMEM((1, 3 * W + LR_PAD), F32),
            pltpu.VMEM((A_PAIRS, LANES, LANES), F32),
        ],
        compiler_params=_cparams(("parallel", "arbitrary")),
        name="rwkv",
    )(proj, proj, proj, proj, sh_rkv, sh_l, s0, *wts)


def _mamba_kernel(z_ref, x_ref, bc_ref, dt_ref, cx0_ref, cbc0_ref, s0_ref,
                  cwx_ref, cwbc_ref, cbx_ref, cbbc_ref, dtb_ref, alog_ref, dsk_ref, nw_ref, e3_ref,
                  yb_ref, ocx_ref, ocbc_ref, os_ref,
                  cx_ref, cbc_ref, s_ref, *, chunk):
    C = chunk
    c_idx = pl.program_id(1)
    GW = B_WIDTH // B_GROUPS
    HD = B_HEAD_DIM
    PR = range(B_PAIRS)
    cols = lambda t, p: t[:, p * LANES:(p + 1) * LANES]

    @pl.when(c_idx == 0)
    def _():
        cx_ref[...] = cx0_ref[0]
        cbc_ref[...] = cbc0_ref[0]
        for p in PR:
            s_ref[:, p * LANES:(p + 1) * LANES] = jnp.concatenate(
                [s0_ref[0, 2 * p].T, s0_ref[0, 2 * p + 1].T], axis=1)

    def causal_conv(raw_ref, carry_ref, taps_ref, bias_ref):
        raw = raw_ref[...]
        ext = jnp.concatenate([carry_ref[...], raw], axis=0)
        acc = bias_ref[...]
        for j in range(CONV_W):
            shifted = ext if j == CONV_W - 1 else pltpu.roll(ext, CONV_W - 1 - j, axis=0)
            acc = acc + taps_ref[j:j + 1, :] * shifted[CONV_ROWS:CONV_ROWS + C, :]
        carry_ref[...] = raw[C - CONV_ROWS:C, :]
        return acc * jax.nn.sigmoid(acc)

    xs = causal_conv(x_ref, cx_ref, cwx_ref, cbx_ref)
    bcm = causal_conv(bc_ref, cbc_ref, cwbc_ref, cbbc_ref)
    ocx_ref[0] = cx_ref[...]
    ocbc_ref[0] = cbc_ref[...]

    lane_c = lax.broadcasted_iota(jnp.int32, (C, LANES), 1)
    valid = lane_c < B_HEADS
    dt = jnp.where(valid, jax.nn.softplus(dt_ref[...] + dtb_ref[...]), 0.0)
    d_a = dt * (-jnp.exp(alog_ref[...]))

    ti3 = lax.broadcasted_iota(jnp.int32, (C, 3 * C), 0)
    sj3 = lax.broadcasted_iota(jnp.int32, (C, 3 * C), 1)
    sj3 = jnp.where(sj3 >= 2 * C, sj3 - 2 * C, jnp.where(sj3 >= C, sj3 - C, sj3))
    tri3 = jnp.where(sj3 <= ti3, 1.0, 0.0).astype(BF16)
    a_cs = _dot(tri3, jnp.concatenate(_split3(d_a), axis=0))

    def pack3(t):
        hi, mid, lo = _split3(t)
        return (hi.astype(F32) + pltpu.roll(mid.astype(F32), B_HEADS, axis=1)
                + pltpu.roll(lo.astype(F32), 2 * B_HEADS, axis=1)).astype(BF16)

    expanded = _dot(jnp.concatenate([pack3(dt), pack3(a_cs)], axis=0), e3_ref[...])
    dt_e = expanded[0:C]
    acs_e = expanded[C:2 * C]
    alast_e = acs_e[C - 1:C, :]

    xdt = xs * dt_e
    xdt_b = xdt.astype(BF16)
    xend_b = (xdt * jnp.exp(alast_e - acs_e)).astype(BF16)
    s_decay = jnp.exp(alast_e)
    in_decay = jnp.exp(acs_e)

    ti2 = lax.broadcasted_iota(jnp.int32, (C, 2 * C), 0)
    lane_tt = lax.broadcasted_iota(jnp.int32, (C, 2 * C), 1)
    sj2 = jnp.where(lane_tt >= C, lane_tt - C, lane_tt)
    causal2 = sj2 <= ti2
    eye2 = sj2 == ti2

    cb2, y_state = [], []
    for g in range(B_GROUPS):
        bm = bcm[:, g * D_STATE:(g + 1) * D_STATE].astype(BF16)
        cm = bcm[:, (B_GROUPS + g) * D_STATE:(B_GROUPS + g + 1) * D_STATE].astype(BF16)
        cb = lax.dot_general(cm, bm, NT, preferred_element_type=F32)
        cb2.append(jnp.concatenate([cb, cb], axis=1))
        gsl = slice(g * GW, (g + 1) * GW)
        s_old = s_ref[:, gsl]
        y_state.append(_dot(cm, s_old.astype(BF16)) * in_decay[:, gsl])
        s_ref[:, gsl] = s_old * s_decay[:, gsl] + lax.dot_general(
            bm, xend_b[:, gsl], TN, preferred_element_type=F32)
    y_state = jnp.concatenate(y_state, axis=1)

    ys = []
    for p in PR:
        if C == HD:
            col_form = cols(acs_e, p)
        else:
            col_form = jnp.concatenate([acs_e[:, p * LANES:p * LANES + C],
                                        acs_e[:, p * LANES + HD:p * LANES + HD + C]], axis=1)
        row_form = jnp.sum(jnp.where(eye2, col_form, 0.0), axis=0, keepdims=True)
        m = jnp.where(causal2, jnp.exp(col_form - row_form), 0.0) * cb2[p // (B_PAIRS // B_GROUPS)]
        ys.append(_dot(m.astype(BF16), _block_diag_rhs(cols(xdt_b, p), lane_c, HD)))
    y = jnp.concatenate(ys, axis=1) + y_state

    z = z_ref[...]
    y = (y + xs * dsk_ref[...]) * (z * jax.nn.sigmoid(z))
    outs = []
    for g in range(B_GROUPS):
        yg = y[:, g * GW:(g + 1) * GW]
        ms = jnp.mean(yg * yg, axis=-1, keepdims=True)
        outs.append(yg * lax.rsqrt(ms + EPS))
    yb_ref[...] = (jnp.concatenate(outs, axis=1) * nw_ref[...]).astype(BF16)

    @pl.when(c_idx == pl.num_programs(1) - 1)
    def _():
        for h in range(B_HEADS):
            os_ref[0, h] = s_ref[:, h * HD:(h + 1) * HD].T


def _mamba(proj, row_off, nb, nc, chunk, cx0, cbc0, s0, shared_state, wts):
    C = chunk
    assert C >= CONV_ROWS and C <= B_HEAD_DIM
    rb = row_off // C
    W = B_WIDTH
    zspec = lambda width, col: pl.BlockSpec((C, width), lambda b, c: (rb + b * nc + c, col // width))
    st = (lambda b: 0) if shared_state else (lambda b: b)
    full = lambda arr: pl.BlockSpec(arr.shape, lambda b, c: (0, 0))
    state_block = (1, B_HEADS, B_HEAD_DIM, D_STATE)
    in_specs = [
        zspec(W, COL_Z), zspec(W, COL_X), zspec(BC_WIDTH, COL_BC), zspec(DT_PAD, COL_DT),
        pl.BlockSpec((1, CONV_ROWS, W), lambda b, c: (st(b), 0, 0)),
        pl.BlockSpec((1, CONV_ROWS, BC_WIDTH), lambda b, c: (st(b), 0, 0)),
        pl.BlockSpec(state_block, lambda b, c: (st(b), 0, 0, 0)),
    ] + [full(w) for w in wts]
    out_specs = [
        pl.BlockSpec((C, W), lambda b, c: (b * nc + c, 0)),
        pl.BlockSpec((1, CONV_ROWS, W), lambda b, c: (b, 0, 0)),
        pl.BlockSpec((1, CONV_ROWS, BC_WIDTH), lambda b, c: (b, 0, 0)),
        pl.BlockSpec(state_block, lambda b, c: (b, 0, 0, 0)),
    ]
    out_shape = [
        jax.ShapeDtypeStruct((nb * nc * C, W), BF16),
        jax.ShapeDtypeStruct((nb, CONV_ROWS, W), F32),
        jax.ShapeDtypeStruct((nb, CONV_ROWS, BC_WIDTH), F32),
        jax.ShapeDtypeStruct((nb,) + state_block[1:], F32),
    ]
    return pl.pallas_call(
        functools.partial(_mamba_kernel, chunk=C),
        grid=(nb, nc),
        in_specs=in_specs,
        out_specs=out_specs,
        out_shape=out_shape,
        scratch_shapes=[
            pltpu.VMEM((CONV_ROWS, W), F32),
            pltpu.VMEM((CONV_ROWS, BC_WIDTH), F32),
            pltpu.VMEM((D_STATE, W), F32),
        ],
        compiler_params=_cparams(("parallel", "arbitrary")),
        name="mamba",
    )(proj, proj, proj, proj, cx0, cbc0, s0, *wts)


def _pad_cols(a, width):
    return jnp.pad(a, [(0, 0)] * (a.ndim - 1) + [(0, width - a.shape[-1])])


def kernel(x_prompt, x_sample, state_rwkv_shift, state_rwkv_wkv, state_ssm_conv, state_ssm, meta_tokens, norm_mix_w, w_in, rwkv_mu, rwkv_w0, rwkv_w_up, rwkv_a0, rwkv_a_up, rwkv_g_up, rwkv_k_k, rwkv_k_a, rwkv_r_k, rwkv_lnx_w, rwkv_lnx_b, ssm_conv_w, ssm_conv_b, ssm_dt_bias, ssm_A_log, ssm_D, ssm_norm_w, w_out, norm_ffn_w, ffn_w_gate, ffn_w_up, ffn_w_down, norm_final_w):
    depth = w_in.shape[0]
    assert depth == 1, "single-layer step"
    bp, lp, d = x_prompt.shape
    bs, ls, _ = x_sample.shape
    n_meta = meta_tokens.shape[0]
    assert d == D_MODEL and lp % CHUNK == 0 and ls <= CHUNK and n_meta <= CHUNK
    assert w_in.shape[-1] == A_COLS + 2 * B_WIDTH + BC_WIDTH + B_HEADS
    W = A_WIDTH

    wi = w_in[0]
    w_perm = jnp.concatenate([
        wi[:, 0:3 * W],
        wi[:, A_COLS:A_COLS + 2 * B_WIDTH + BC_WIDTH],
        _pad_cols(wi[:, 3 * W:A_COLS], LR_PAD),
        _pad_cols(wi[:, A_COLS + 2 * B_WIDTH + BC_WIDTH:], DT_PAD),
    ], axis=1).astype(BF16)
    nmw = norm_mix_w.reshape(1, D_MODEL)
    mu = rwkv_mu[0]
    mu_rkv = mu[None, 0:3 * W]
    mu_l = _pad_cols(mu[None, 3 * W:], LR_PAD)
    wa_up = jnp.zeros((LANES, 2 * W), F32)
    wa_up = wa_up.at[0:DECAY_RANK, 0:W].set(rwkv_w_up[0])
    wa_up = wa_up.at[DECAY_RANK:DECAY_RANK + AAA_RANK, W:].set(rwkv_a_up[0]).astype(BF16)
    g_up = jnp.pad(rwkv_g_up[0], ((0, 2 * LANES - GATE_RANK), (0, 0))).astype(BF16)
    lane_head = jnp.arange(LANES, dtype=jnp.int32) // A_HEAD_DIM
    ones_bd = (lane_head[:, None] == lane_head[None, :]).astype(BF16)
    rwkv_wts = (mu_rkv, mu_l, rwkv_w0.reshape(1, W), rwkv_a0.reshape(1, W), wa_up, g_up,
                rwkv_k_k.reshape(1, W), rwkv_k_a.reshape(1, W), rwkv_r_k.reshape(1, W),
                rwkv_lnx_w.reshape(1, W), rwkv_lnx_b.reshape(1, W), ones_bd)
    taps = ssm_conv_w[0].T
    cbias = ssm_conv_b.reshape(1, -1)
    src_head = jnp.where(jnp.arange(LANES) < 3 * B_HEADS, jnp.arange(LANES) % B_HEADS, -1)
    e3 = (src_head[:, None] == (jnp.arange(B_WIDTH) // B_HEAD_DIM)[None, :]).astype(BF16)
    mamba_wts = (taps[:, :B_WIDTH], taps[:, B_WIDTH:], cbias[:, :B_WIDTH], cbias[:, B_WIDTH:],
                 _pad_cols(ssm_dt_bias.reshape(1, B_HEADS), DT_PAD),
                 _pad_cols(ssm_A_log.reshape(1, B_HEADS), DT_PAD),
                 jnp.repeat(ssm_D.reshape(1, B_HEADS), B_HEAD_DIM, axis=1),
                 ssm_norm_w.reshape(1, B_WIDTH), e3)
    wo = w_out[0].astype(BF16)
    wo_a, wo_b = wo[:W], wo[W:]
    nfw = norm_ffn_w.reshape(1, D_MODEL)
    wg = ffn_w_gate[0].astype(BF16)
    wu = ffn_w_up[0].astype(BF16)
    wd = ffn_w_down[0].astype(BF16)
    nlw = norm_final_w.reshape(1, D_MODEL)

    tm_small = 384
    n_small = bs * ls + n_meta
    m_small = -(-n_small // tm_small) * tm_small
    x_small = jnp.concatenate([x_sample.reshape(bs * ls, d), meta_tokens,
                               jnp.zeros((m_small - n_small, d), F32)], axis=0)
    x_big = x_prompt.reshape(bp * lp, d)
    tm_big = math.gcd(bp * lp, 512)

    proj_small = _proj(x_small, nmw, w_perm, m_small)
    proj_big = _proj(x_big, nmw, w_perm, math.gcd(bp * lp, 1024))

    zeros = lambda *s: jnp.zeros(s, F32)
    sh = state_rwkv_shift[0]
    smp_rwkv = (sh[:, None, 0:3 * W], _pad_cols(sh[:, None, 3 * W:], LR_PAD), state_rwkv_wkv[0])
    conv0 = jnp.pad(state_ssm_conv[0], ((0, 0), (CONV_ROWS - (CONV_W - 1), 0), (0, 0)))
    smp_mamba = (conv0[..., :B_WIDTH], conv0[..., B_WIDTH:], state_ssm[0])
    meta_rwkv0 = (zeros(1, 1, 3 * W), zeros(1, 1, LR_PAD), zeros(1, A_HEADS, A_HEAD_DIM, A_HEAD_DIM))
    meta_mamba0 = (zeros(1, CONV_ROWS, B_WIDTH), zeros(1, CONV_ROWS, BC_WIDTH),
                   zeros(1, B_HEADS, B_HEAD_DIM, D_STATE))

    ya_m, *meta_rwkv = _rwkv(proj_small, bs * ls, 1, 1, n_meta, *meta_rwkv0, False, rwkv_wts)
    yb_m, *meta_mamba = _mamba(proj_small, bs * ls, 1, 1, n_meta, *meta_mamba0, False, mamba_wts)
    ya_s, *s_rwkv = _rwkv(proj_small, 0, bs, 1, ls, *smp_rwkv, False, rwkv_wts)
    yb_s, *s_mamba = _mamba(proj_small, 0, bs, 1, ls, *smp_mamba, False, mamba_wts)
    ya_p, *p_rwkv = _rwkv(proj_big, 0, bp, lp // CHUNK, CHUNK, *meta_rwkv, True, rwkv_wts)
    yb_p, *p_mamba = _mamba(proj_big, 0, bp, lp // CHUNK, CHUNK, *meta_mamba, True, mamba_wts)

    def tail(ya, yb, x, tm, tm_ffn1):
        x1, h2 = _outproj(x, ya, yb, wo_a, wo_b, nfw, tm)
        g = _ffn1(h2, wg, wu, tm_ffn1, 512)
        return _ffn2(g, wd, x1, nlw, tm, D_FF // 2)

    pad_rows = lambda *parts: jnp.concatenate(
        list(parts) + [jnp.zeros((m_small - n_small, parts[0].shape[1]), parts[0].dtype)], axis=0)
    y_small = tail(pad_rows(ya_s, ya_m), pad_rows(yb_s, yb_m), x_small, tm_small, m_small)
    y_big = tail(ya_p, yb_p, x_big, tm_big, tm_big)

    def states_out(rw, mb):
        sh_rkv, sh_l, wkv = rw
        cx, cbc, ssm = mb
        shift = jnp.concatenate([sh_rkv[:, 0], sh_l[:, 0, :LOWRANK]], axis=-1)[None]
        conv = jnp.concatenate([cx, cbc], axis=-1)[None, :, CONV_ROWS - (CONV_W - 1):]
        return shift, wkv[None], conv, ssm[None]

    y_prompt = y_big.reshape(bp, lp, d)
    y_sample = y_small[:bs * ls].reshape(bs, ls, d)
    return (y_prompt, y_sample) + states_out(p_rwkv, p_mamba) + states_out(s_rwkv, s_mamba)
```

```python
import functools
import math

import jax
import jax.numpy as jnp
from jax import lax
from jax.experimental import pallas as pl
from jax.experimental.pallas import tpu as pltpu

F32 = jnp.float32
BF16 = jnp.bfloat16

D_MODEL = 2048
CHUNK = 64
EPS = 1e-6
LNX_EPS = 64e-5
A_WIDTH = 1024
A_HEADS = 16
A_HEAD_DIM = 64
A_PAIRS = A_HEADS // 2
DECAY_RANK = 64
AAA_RANK = 64
GATE_RANK = 160
B_WIDTH = 1024
B_HEADS = 16
B_HEAD_DIM = 64
B_PAIRS = B_HEADS // 2
B_GROUPS = 2
D_STATE = 128
CONV_W = 4
BC_WIDTH = 2 * B_GROUPS * D_STATE
A_COLS = 3 * A_WIDTH + DECAY_RANK + AAA_RANK + GATE_RANK
LOWRANK = DECAY_RANK + AAA_RANK + GATE_RANK
D_FF = 5632

LANES = 128
SUBLANES = 8
VMEM_LIMIT = 56 * 1024 * 1024

LR_PAD = 512
DT_PAD = 128
DT_COLS = 256
COL_R, COL_K, COL_V, COL_Z, COL_X = 0, 1024, 2048, 3072, 4096
COL_BC = 5120
COL_LR = 5632
COL_DT = 6144
PROJ_COLS = COL_DT + DT_COLS
PROJ_TN = 1280
CONV_ROWS = SUBLANES

NT = (((1,), (1,)), ((), ()))
TN = (((0,), (0,)), ((), ()))


def _cparams(sem):
    return pltpu.CompilerParams(dimension_semantics=sem, vmem_limit_bytes=VMEM_LIMIT)


def _dot(x, y):
    return jnp.dot(x, y, preferred_element_type=F32)


def _block_diag_rhs(x, lane, split):
    zero = jnp.zeros_like(x)
    return jnp.concatenate([jnp.where(lane < split, x, zero), jnp.where(lane < split, zero, x)], axis=0)


def _split3(x):
    hi = x.astype(BF16)
    r1 = x - hi.astype(F32)
    mid = r1.astype(BF16)
    return hi, mid, (r1 - mid.astype(F32)).astype(BF16)


def _proj_kernel(x_ref, nw_ref, w_ref, o_ref, h_ref):
    @pl.when(pl.program_id(1) == 0)
    def _():
        x = x_ref[...]
        ms = jnp.mean(x * x, axis=-1, keepdims=True)
        h_ref[...] = (x * lax.rsqrt(ms + EPS) * nw_ref[...]).astype(BF16)

    o_ref[...] = _dot(h_ref[...], w_ref[...])


def _proj(x, norm_w, w, tm):
    m = x.shape[0]
    return pl.pallas_call(
        _proj_kernel,
        grid=(m // tm, PROJ_COLS // PROJ_TN),
        in_specs=[
            pl.BlockSpec((tm, D_MODEL), lambda i, j: (i, 0)),
            pl.BlockSpec((1, D_MODEL), lambda i, j: (0, 0)),
            pl.BlockSpec((D_MODEL, PROJ_TN), lambda i, j: (0, j)),
        ],
        out_specs=pl.BlockSpec((tm, PROJ_TN), lambda i, j: (i, j)),
        out_shape=jax.ShapeDtypeStruct((m, PROJ_COLS), F32),
        scratch_shapes=[pltpu.VMEM((tm, D_MODEL), BF16)],
        compiler_params=_cparams(("parallel", "arbitrary")),
        name="proj",
    )(x, norm_w, w)


def _outproj_kernel(x_ref, ya_ref, yb_ref, wa_ref, wb_ref, nw_ref, x1_ref, h2_ref):
    x1 = x_ref[...] + _dot(ya_ref[...], wa_ref[...]) + _dot(yb_ref[...], wb_ref[...])
    x1_ref[...] = x1
    ms = jnp.mean(x1 * x1, axis=-1, keepdims=True)
    h2_ref[...] = (x1 * lax.rsqrt(ms + EPS) * nw_ref[...]).astype(BF16)


def _outproj(x, ya, yb, wa, wb, norm_w, tm):
    m = x.shape[0]
    row = lambda i: (i, 0)
    fixed = lambda i: (0, 0)
    return pl.pallas_call(
        _outproj_kernel,
        grid=(m // tm,),
        in_specs=[
            pl.BlockSpec((tm, D_MODEL), row),
            pl.BlockSpec((tm, A_WIDTH), row),
            pl.BlockSpec((tm, B_WIDTH), row),
            pl.BlockSpec((A_WIDTH, D_MODEL), fixed),
            pl.BlockSpec((B_WIDTH, D_MODEL), fixed),
            pl.BlockSpec((1, D_MODEL), fixed),
        ],
        out_specs=[pl.BlockSpec((tm, D_MODEL), row), pl.BlockSpec((tm, D_MODEL), row)],
        out_shape=[jax.ShapeDtypeStruct((m, D_MODEL), F32),
                   jax.ShapeDtypeStruct((m, D_MODEL), BF16)],
        compiler_params=_cparams(("parallel",)),
        name="outproj",
    )(x, ya, yb, wa, wb, norm_w)


def _ffn1_kernel(h_ref, wg_ref, wu_ref, o_ref):
    h = h_ref[...]
    g = _dot(h, wg_ref[...])
    u = _dot(h, wu_ref[...])
    o_ref[...] = (g * jax.nn.sigmoid(g) * u).astype(BF16)


def _ffn1(h2, wg, wu, tm, tn):
    m = h2.shape[0]
    return pl.pallas_call(
        _ffn1_kernel,
        grid=(m // tm, D_FF // tn),
        in_specs=[
            pl.BlockSpec((tm, D_MODEL), lambda i, j: (i, 0)),
            pl.BlockSpec((D_MODEL, tn), lambda i, j: (0, j)),
            pl.BlockSpec((D_MODEL, tn), lambda i, j: (0, j)),
        ],
        out_specs=pl.BlockSpec((tm, tn), lambda i, j: (i, j)),
        out_shape=jax.ShapeDtypeStruct((m, D_FF), BF16),
        compiler_params=_cparams(("parallel", "parallel")),
        name="ffn1",
    )(h2, wg, wu)


def _ffn2_kernel(g_ref, wd_ref, x1_ref, nw_ref, o_ref):
    k = pl.program_id(1)

    @pl.when(k == 0)
    def _():
        o_ref[...] = x1_ref[...] + _dot(g_ref[...], wd_ref[...])

    @pl.when(jnp.logical_and(k > 0, k < pl.num_programs(1) - 1))
    def _():
        o_ref[...] += _dot(g_ref[...], wd_ref[...])

    @pl.when(k == pl.num_programs(1) - 1)
    def _():
        x2 = o_ref[...] + _dot(g_ref[...], wd_ref[...])
        ms = jnp.mean(x2 * x2, axis=-1, keepdims=True)
        o_ref[...] = x2 * lax.rsqrt(ms + EPS) * nw_ref[...]


def _ffn2(g, wd, x1, norm_w, tm, tk):
    m = g.shape[0]
    assert D_FF // tk >= 2
    return pl.pallas_call(
        _ffn2_kernel,
        grid=(m // tm, D_FF // tk),
        in_specs=[
            pl.BlockSpec((tm, tk), lambda i, k: (i, k)),
            pl.BlockSpec((tk, D_MODEL), lambda i, k: (k, 0)),
            pl.BlockSpec((tm, D_MODEL), lambda i, k: (i, 0)),
            pl.BlockSpec((1, D_MODEL), lambda i, k: (0, 0)),
        ],
        out_specs=pl.BlockSpec((tm, D_MODEL), lambda i, k: (i, 0)),
        out_shape=jax.ShapeDtypeStruct((m, D_MODEL), F32),
        compiler_params=_cparams(("parallel", "arbitrary")),
        name="ffn2",
    )(g, wd, x1, norm_w)


def _shift_rows(z, prev_row):
    rolled = pltpu.roll(z, 1, axis=0)
    row = lax.broadcasted_iota(jnp.int32, z.shape, 0)
    return jnp.where(row == 0, prev_row, rolled)


def _rwkv_kernel(zr_ref, zk_ref, zv_ref, zl_ref, sh_rkv_ref, sh_l_ref, s0_ref,
                 mu_rkv_ref, mu_l_ref, w0_ref, a0_ref, wa_up_ref, g_up_ref,
                 kk_ref, ka_ref, rk_ref, lnw_ref, lnb_ref, ones_ref,
                 ya_ref, osh_rkv_ref, osh_l_ref, os_ref,
                 prev_ref, s_ref, ops_ref, aux_ref, plast_ref, *, chunk):
    C = chunk
    W = A_WIDTH
    HD = A_HEAD_DIM
    c_idx = pl.program_id(1)
    wslot = c_idx % 2
    rslot = 1 - wslot
    PR = range(A_PAIRS)
    cols = lambda t, p: t[:, p * LANES:(p + 1) * LANES]
    OP_AT, OP_RT, OP_KH, OP_BH, OP_KE, OP_BE, OP_V = range(7)
    AUX_BONUS, AUX_GATE = range(2)

    @pl.when(c_idx == 0)
    def _():
        prev_ref[:, 0:3 * W] = sh_rkv_ref[0]
        prev_ref[:, 3 * W:3 * W + LR_PAD] = sh_l_ref[0]
        zero = jnp.zeros((HD, HD), F32)
        for p in PR:
            s_ref[p] = jnp.concatenate(
                [jnp.concatenate([s0_ref[0, 2 * p], zero], axis=1),
                 jnp.concatenate([zero, s0_ref[0, 2 * p + 1]], axis=1)], axis=0)
        ops_ref[1] = jnp.zeros(ops_ref.shape[1:], BF16)
        aux_ref[1] = jnp.zeros(aux_ref.shape[1:], F32)
        plast_ref[1] = jnp.ones(plast_ref.shape[1:], F32)

    ti2 = lax.broadcasted_iota(jnp.int32, (C, 2 * C), 0)
    lane_tt = lax.broadcasted_iota(jnp.int32, (C, 2 * C), 1)
    sj2 = jnp.where(lane_tt >= C, lane_tt - C, lane_tt)
    incl2 = sj2 <= ti2
    strict2 = sj2 < ti2
    tri2_b = jnp.where(incl2, 1.0, 0.0).astype(BF16)
    lane_c = lax.broadcasted_iota(jnp.int32, (C, LANES), 1)
    bd_i = lax.broadcasted_iota(jnp.int32, (LANES, LANES), 0) // HD
    bd_j = lax.broadcasted_iota(jnp.int32, (LANES, LANES), 1) // HD
    bd_mask = bd_i == bd_j
    n_sq = int(math.log2(C)) - 1
    bd_ch = lambda x: _block_diag_rhs(x, lane_c, HD)
    bd_tt = lambda x: _block_diag_rhs(x, lane_tt, C)

    def mixed(z, prev_lo, mu):
        width = z.shape[1]
        zp = _shift_rows(z, prev_ref[:, prev_lo:prev_lo + width])
        prev_ref[:, prev_lo:prev_lo + width] = z[C - 1:C, :]
        return z + mu * (zp - z)

    zl_raw = zl_ref[...]
    zl = mixed(zl_raw, 3 * W, mu_l_ref[...])
    osh_l_ref[0] = zl_raw[C - 1:C, :]
    l1 = zl[:, 0:LANES]
    l1 = jnp.where(lane_c < DECAY_RANK, jnp.tanh(l1), l1)
    wa = _dot(l1.astype(BF16), wa_up_ref[...])
    gl = jax.nn.sigmoid(zl[:, LANES:3 * LANES])
    aux_ref[wslot, AUX_GATE] = _dot(gl.astype(BF16), g_up_ref[...])

    def hsum_tiles(tiles):
        s = _dot(jnp.concatenate(tiles, axis=0).astype(BF16), ones_ref[...])
        return [s[i * C:(i + 1) * C] for i in range(len(tiles))]

    def prepare_tile(p):
        lo, hi = p * LANES, (p + 1) * LANES
        sl = slice(lo, hi)

        def mixed_rkv(z_ref, base):
            z = z_ref[:, sl]
            osh_rkv_ref[0, :, base + lo:base + hi] = z[C - 1:C, :]
            return mixed(z, base + lo, mu_rkv_ref[:, base + lo:base + hi])

        hsum = lambda x: hsum_tiles([x])[0]
        r = mixed_rkv(zr_ref, 0)
        k = mixed_rkv(zk_ref, W)
        v = mixed_rkv(zv_ref, 2 * W)
        wlog = -jax.nn.softplus(-(w0_ref[:, sl] + wa[:, sl])) - 0.5
        lw = -jnp.exp(wlog)
        a = jax.nn.sigmoid(a0_ref[:, sl] + wa[:, W + lo:W + hi])
        kk = k * kk_ref[:, sl]
        k2 = k * (1.0 + (a - 1.0) * ka_ref[:, sl])
        kk = kk * lax.rsqrt(jnp.maximum(hsum(kk * kk), 1e-24))
        ka = kk * a
        lw_hi = lw.astype(BF16)
        lw_lo = (lw - lw_hi.astype(F32)).astype(BF16)
        cum = _dot(tri2_b, jnp.concatenate([lw_hi, lw_lo], axis=0))
        cum_last = cum[C - 1:C, :]
        e_neg = jnp.exp(-cum)
        e_end = jnp.exp(cum_last - cum)
        ops_ref[wslot, OP_AT, :, sl] = (kk * jnp.exp(cum - lw)).astype(BF16)
        ops_ref[wslot, OP_RT, :, sl] = (r * jnp.exp(cum)).astype(BF16)
        ops_ref[wslot, OP_KH, :, sl] = (k2 * e_neg).astype(BF16)
        ops_ref[wslot, OP_BH, :, sl] = (ka * e_neg).astype(BF16)
        ops_ref[wslot, OP_KE, :, sl] = (k2 * e_end).astype(BF16)
        ops_ref[wslot, OP_BE, :, sl] = (-ka * e_end).astype(BF16)
        ops_ref[wslot, OP_V, :, sl] = v.astype(BF16)
        plast_ref[wslot, :, sl] = jnp.exp(cum_last)
        aux_ref[wslot, AUX_BONUS, :, sl] = hsum(r * k2 * rk_ref[:, sl]) * v

    pending = list(PR)

    def prepare_next():
        if pending:
            prepare_tile(pending.pop(0))

    op = lambda i, p: ops_ref[rslot, i, :, p * LANES:(p + 1) * LANES]
    s_old = [s_ref[p] for p in PR]
    v_b = [op(OP_V, p) for p in PR]
    lhs = [jnp.concatenate([op(OP_AT, p), op(OP_RT, p)], axis=0) for p in PR]
    rhs = [jnp.concatenate([bd_ch(op(OP_KH, p)), bd_ch(op(OP_BH, p))], axis=0) for p in PR]
    upd_rhs = [jnp.concatenate([op(OP_KE, p), op(OP_BE, p)], axis=0) for p in PR]
    p_last = plast_ref[rslot]
    m1 = [lax.dot_general(lhs[p], rhs[p], NT, preferred_element_type=F32) for p in PR]
    m2 = [lax.dot_general(lhs[p], s_old[p].astype(BF16), NT, preferred_element_type=F32) for p in PR]
    prepare_next()
    a_k = [jnp.where(strict2, m1[p][0:C, 0:2 * C], 0.0).astype(BF16) for p in PR]
    a_b = [jnp.where(strict2, m1[p][0:C, 2 * C:4 * C], 0.0) for p in PR]
    b_kb = [jnp.concatenate([jnp.where(incl2, m1[p][C:2 * C, 0:2 * C], 0.0),
                             jnp.where(incl2, -m1[p][C:2 * C, 2 * C:4 * C], 0.0)], axis=1).astype(BF16)
            for p in PR]
    bd_v = [bd_ch(v_b[p]) for p in PR]
    u = [m2[p][0:C] + _dot(a_k[p], bd_v[p]) for p in PR]
    prepare_next()
    pw_b = [a_b[p].astype(BF16) for p in PR]
    u = [u[p] - _dot(pw_b[p], bd_ch(u[p].astype(BF16))) for p in PR]
    prepare_next()
    for _ in range(n_sq):
        pw_b = [_dot(pw_b[p], bd_tt(pw_b[p])).astype(BF16) for p in PR]
        u = [u[p] + _dot(pw_b[p], bd_ch(u[p].astype(BF16))) for p in PR]
        prepare_next()
    u_b = [u[p].astype(BF16) for p in PR]
    ys = [m2[p][C:2 * C] + _dot(b_kb[p], jnp.concatenate([bd_v[p], bd_ch(u_b[p])], axis=0)) for p in PR]
    upd = [lax.dot_general(jnp.concatenate([v_b[p], u_b[p]], axis=0), upd_rhs[p],
                           TN, preferred_element_type=F32) for p in PR]
    for p in PR:
        s_ref[p] = s_old[p] * cols(p_last, p) + jnp.where(bd_mask, upd[p], 0.0)
    while pending:
        prepare_next()

    inv_n = 1.0 / A_HEAD_DIM
    mean = hsum_tiles(ys)
    d = [ys[p] - mean[p] * inv_n for p in PR]
    var = hsum_tiles([d[p] * d[p] for p in PR])
    yn = jnp.concatenate([d[p] * lax.rsqrt(var[p] * inv_n + LNX_EPS) for p in PR], axis=1)
    y = yn * lnw_ref[...] + lnb_ref[...] + aux_ref[rslot, AUX_BONUS]
    ya_ref[...] = (y * aux_ref[rslot, AUX_GATE]).astype(BF16)

    @pl.when(c_idx == pl.num_programs(1) - 1)
    def _():
        for p in PR:
            s_pair = s_ref[p]
            os_ref[0, 2 * p] = s_pair[0:HD, 0:HD]
            os_ref[0, 2 * p + 1] = s_pair[HD:2 * HD, HD:2 * HD]


def _rwkv(proj, row_off, nb, nc, chunk, sh_rkv, sh_l, s0, shared_state, wts):
    C = chunk
    rb = row_off // C
    W = A_WIDTH
    zspec = lambda width, col: pl.BlockSpec(
        (C, width), lambda b, c: (rb + b * nc + jnp.minimum(c, nc - 1), col // width))
    st = (lambda b: 0) if shared_state else (lambda b: b)
    full = lambda arr: pl.BlockSpec(arr.shape, lambda b, c: (0, 0))
    state_block = (1, A_HEADS, A_HEAD_DIM, A_HEAD_DIM)
    in_specs = [
        zspec(W, COL_R), zspec(W, COL_K), zspec(W, COL_V), zspec(LR_PAD, COL_LR),
        pl.BlockSpec((1, 1, 3 * W), lambda b, c: (st(b), 0, 0)),
        pl.BlockSpec((1, 1, LR_PAD), lambda b, c: (st(b), 0, 0)),
        pl.BlockSpec(state_block, lambda b, c: (st(b), 0, 0, 0)),
    ] + [full(w) for w in wts]
    out_specs = [
        pl.BlockSpec((C, W), lambda b, c: (b * nc + jnp.maximum(c - 1, 0), 0)),
        pl.BlockSpec((1, 1, 3 * W), lambda b, c: (b, 0, 0)),
        pl.BlockSpec((1, 1, LR_PAD), lambda b, c: (b, 0, 0)),
        pl.BlockSpec(state_block, lambda b, c: (b, 0, 0, 0)),
    ]
    out_shape = [
        jax.ShapeDtypeStruct((nb * nc * C, W), BF16),
        jax.ShapeDtypeStruct((nb, 1, 3 * W), F32),
        jax.ShapeDtypeStruct((nb, 1, LR_PAD), F32),
        jax.ShapeDtypeStruct((nb,) + state_block[1:], F32),
    ]
    return pl.pallas_call(
        functools.partial(_rwkv_kernel, chunk=C),
        grid=(nb, nc + 1),
        in_specs=in_specs,
        out_specs=out_specs,
        out_shape=out_shape,
        scratch_shapes=[
            pltpu.VMEM((1, 3 * W + LR_PAD), F32),
            pltpu.VMEM((A_PAIRS, LANES, LANES), F32),
            pltpu.VMEM((2, 7, C, W), BF16),
            pltpu.VMEM((2, 2, C, W), F32),
            pltpu.VMEM((2, 1, W), F32),
        ],
        compiler_params=_cparams(("parallel", "arbitrary")),
        name="rwkv",
    )(proj, proj, proj, proj, sh_rkv, sh_l, s0, *wts)


def _mamba_kernel(z_ref, x_ref, bc_ref, dt_ref, cx0_ref, cbc0_ref, s0_ref,
                  cwx_ref, cwbc_ref, cbx_ref, cbbc_ref, dtb_ref, alog_ref, dsk_ref, nw_ref, e3_ref,
                  yb_ref, ocx_ref, ocbc_ref, os_ref,
                  cx_ref, cbc_ref, s_ref, *, chunk):
    C = chunk
    c_idx = pl.program_id(1)
    GW = B_WIDTH // B_GROUPS
    HD = B_HEAD_DIM
    PR = range(B_PAIRS)
    cols = lambda t, p: t[:, p * LANES:(p + 1) * LANES]

    @pl.when(c_idx == 0)
    def _():
        cx_ref[...] = cx0_ref[0]
        cbc_ref[...] = cbc0_ref[0]
        for p in PR:
            s_ref[:, p * LANES:(p + 1) * LANES] = jnp.concatenate(
                [s0_ref[0, 2 * p].T, s0_ref[0, 2 * p + 1].T], axis=1)

    def causal_conv(raw_ref, carry_ref, taps_ref, bias_ref):
        raw = raw_ref[...]
        ext = jnp.concatenate([carry_ref[...], raw], axis=0)
        acc = bias_ref[...]
        for j in range(CONV_W):
            shifted = ext if j == CONV_W - 1 else pltpu.roll(ext, CONV_W - 1 - j, axis=0)
            acc = acc + taps_ref[j:j + 1, :] * shifted[CONV_ROWS:CONV_ROWS + C, :]
        carry_ref[...] = raw[C - CONV_ROWS:C, :]
        return acc * jax.nn.sigmoid(acc)

    xs = causal_conv(x_ref, cx_ref, cwx_ref, cbx_ref)
    bcm = causal_conv(bc_ref, cbc_ref, cwbc_ref, cbbc_ref)
    ocx_ref[0] = cx_ref[...]
    ocbc_ref[0] = cbc_ref[...]

    lane_c = lax.broadcasted_iota(jnp.int32, (C, LANES), 1)
    valid = lane_c < B_HEADS
    dt = jnp.where(valid, jax.nn.softplus(dt_ref[...] + dtb_ref[...]), 0.0)
    d_a = dt * (-jnp.exp(alog_ref[...]))

    ti3 = lax.broadcasted_iota(jnp.int32, (C, 3 * C), 0)
    sj3 = lax.broadcasted_iota(jnp.int32, (C, 3 * C), 1)
    sj3 = jnp.where(sj3 >= 2 * C, sj3 - 2 * C, jnp.where(sj3 >= C, sj3 - C, sj3))
    tri3 = jnp.where(sj3 <= ti3, 1.0, 0.0).astype(BF16)
    a_cs = _dot(tri3, jnp.concatenate(_split3(d_a), axis=0))

    def pack3(t):
        hi, mid, lo = _split3(t)
        return (hi.astype(F32) + pltpu.roll(mid.astype(F32), B_HEADS, axis=1)
                + pltpu.roll(lo.astype(F32), 2 * B_HEADS, axis=1)).astype(BF16)

    expanded = _dot(jnp.concatenate([pack3(dt), pack3(a_cs)], axis=0), e3_ref[...])
    dt_e = expanded[0:C]
    acs_e = expanded[C:2 * C]
    alast_e = acs_e[C - 1:C, :]

    xdt = xs * dt_e
    xdt_b = xdt.astype(BF16)
    xend_b = (xdt * jnp.exp(alast_e - acs_e)).astype(BF16)
    s_decay = jnp.exp(alast_e)
    in_decay = jnp.exp(acs_e)

    ti2 = lax.broadcasted_iota(jnp.int32, (C, 2 * C), 0)
    lane_tt = lax.broadcasted_iota(jnp.int32, (C, 2 * C), 1)
    sj2 = jnp.where(lane_tt >= C, lane_tt - C, lane_tt)
    causal2 = sj2 <= ti2
    eye2 = sj2 == ti2

    cb2, y_state = [], []
    for g in range(B_GROUPS):
        bm = bcm[:, g * D_STATE:(g + 1) * D_STATE].astype(BF16)
        cm = bcm[:, (B_GROUPS + g) * D_STATE:(B_GROUPS + g + 1) * D_STATE].astype(BF16)
        cb = lax.dot_general(cm, bm, NT, preferred_element_type=F32)
        cb2.append(jnp.concatenate([cb, cb], axis=1))
        gsl = slice(g * GW, (g + 1) * GW)
        s_old = s_ref[:, gsl]
        y_state.append(_dot(cm, s_old.astype(BF16)) * in_decay[:, gsl])
        s_ref[:, gsl] = s_old * s_decay[:, gsl] + lax.dot_general(
            bm, xend_b[:, gsl], TN, preferred_element_type=F32)
    y_state = jnp.concatenate(y_state, axis=1)

    ys = []
    for p in PR:
        if C == HD:
            col_form = cols(acs_e, p)
        else:
            col_form = jnp.concatenate([acs_e[:, p * LANES:p * LANES + C],
                                        acs_e[:, p * LANES + HD:p * LANES + HD + C]], axis=1)
        row_form = jnp.sum(jnp.where(eye2, col_form, 0.0), axis=0, keepdims=True)
        m = jnp.where(causal2, jnp.exp(col_form - row_form), 0.0) * cb2[p // (B_PAIRS // B_GROUPS)]
        ys.append(_dot(m.astype(BF16), _block_diag_rhs(cols(xdt_b, p), lane_c, HD)))
    y = jnp.concatenate(ys, axis=1) + y_state

    z = z_ref[...]
    y = (y + xs * dsk_ref[...]) * (z * jax.nn.sigmoid(z))
    outs = []
    for g in range(B_GROUPS):
        yg = y[:, g * GW:(g + 1) * GW]
        ms = jnp.mean(yg * yg, axis=-1, keepdims=True)
        outs.append(yg * lax.rsqrt(ms + EPS))
    yb_ref[...] = (jnp.concatenate(outs, axis=1) * nw_ref[...]).astype(BF16)

    @pl.when(c_idx == pl.num_programs(1) - 1)
    def _():
        for h in range(B_HEADS):
            os_ref[0, h] = s_ref[:, h * HD:(h + 1) * HD].T


def _mamba(proj, row_off, nb, nc, chunk, cx0, cbc0, s0, shared_state, wts):
    C = chunk
    assert C >= CONV_ROWS and C <= B_HEAD_DIM
    rb = row_off // C
    W = B_WIDTH
    zspec = lambda width, col: pl.BlockSpec((C, width), lambda b, c: (rb + b * nc + c, col // width))
    st = (lambda b: 0) if shared_state else (lambda b: b)
    full = lambda arr: pl.BlockSpec(arr.shape, lambda b, c: (0, 0))
    state_block = (1, B_HEADS, B_HEAD_DIM, D_STATE)
    in_specs = [
        zspec(W, COL_Z), zspec(W, COL_X), zspec(BC_WIDTH, COL_BC), zspec(DT_PAD, COL_DT),
        pl.BlockSpec((1, CONV_ROWS, W), lambda b, c: (st(b), 0, 0)),
        pl.BlockSpec((1, CONV_ROWS, BC_WIDTH), lambda b, c: (st(b), 0, 0)),
        pl.BlockSpec(state_block, lambda b, c: (st(b), 0, 0, 0)),
    ] + [full(w) for w in wts]
    out_specs = [
        pl.BlockSpec((C, W), lambda b, c: (b * nc + c, 0)),
        pl.BlockSpec((1, CONV_ROWS, W), lambda b, c: (b, 0, 0)),
        pl.BlockSpec((1, CONV_ROWS, BC_WIDTH), lambda b, c: (b, 0, 0)),
        pl.BlockSpec(state_block, lambda b, c: (b, 0, 0, 0)),
    ]
    out_shape = [
        jax.ShapeDtypeStruct((nb * nc * C, W), BF16),
        jax.ShapeDtypeStruct((nb, CONV_ROWS, W), F32),
        jax.ShapeDtypeStruct((nb, CONV_ROWS, BC_WIDTH), F32),
        jax.ShapeDtypeStruct((nb,) + state_block[1:], F32),
    ]
    return pl.pallas_call(
        functools.partial(_mamba_kernel, chunk=C),
        grid=(nb, nc),
        in_specs=in_specs,
        out_specs=out_specs,
        out_shape=out_shape,
        scratch_shapes=[
            pltpu.VMEM((CONV_ROWS, W), F32),
            pltpu.VMEM((CONV_ROWS, BC_WIDTH), F32),
            pltpu.VMEM((D_STATE, W), F32),
        ],
        compiler_params=_cparams(("parallel", "arbitrary")),
        name="mamba",
    )(proj, proj, proj, proj, cx0, cbc0, s0, *wts)


def _pad_cols(a, width):
    return jnp.pad(a, [(0, 0)] * (a.ndim - 1) + [(0, width - a.shape[-1])])


def kernel(x_prompt, x_sample, state_rwkv_shift, state_rwkv_wkv, state_ssm_conv, state_ssm, meta_tokens, norm_mix_w, w_in, rwkv_mu, rwkv_w0, rwkv_w_up, rwkv_a0, rwkv_a_up, rwkv_g_up, rwkv_k_k, rwkv_k_a, rwkv_r_k, rwkv_lnx_w, rwkv_lnx_b, ssm_conv_w, ssm_conv_b, ssm_dt_bias, ssm_A_log, ssm_D, ssm_norm_w, w_out, norm_ffn_w, ffn_w_gate, ffn_w_up, ffn_w_down, norm_final_w):
    depth = w_in.shape[0]
    assert depth == 1, "single-layer step"
    bp, lp, d = x_prompt.shape
    bs, ls, _ = x_sample.shape
    n_meta = meta_tokens.shape[0]
    assert d == D_MODEL and lp % CHUNK == 0 and ls <= CHUNK and n_meta <= CHUNK
    assert w_in.shape[-1] == A_COLS + 2 * B_WIDTH + BC_WIDTH + B_HEADS
    W = A_WIDTH

    wi = w_in[0]
    w_perm = jnp.concatenate([
        wi[:, 0:3 * W],
        wi[:, A_COLS:A_COLS + 2 * B_WIDTH + BC_WIDTH],
        _pad_cols(wi[:, 3 * W:A_COLS], LR_PAD),
        _pad_cols(wi[:, A_COLS + 2 * B_WIDTH + BC_WIDTH:], DT_COLS),
    ], axis=1).astype(BF16)
    nmw = norm_mix_w.reshape(1, D_MODEL)
    mu = rwkv_mu[0]
    mu_rkv = mu[None, 0:3 * W]
    mu_l = _pad_cols(mu[None, 3 * W:], LR_PAD)
    wa_up = jnp.zeros((LANES, 2 * W), F32)
    wa_up = wa_up.at[0:DECAY_RANK, 0:W].set(rwkv_w_up[0])
    wa_up = wa_up.at[DECAY_RANK:DECAY_RANK + AAA_RANK, W:].set(rwkv_a_up[0]).astype(BF16)
    g_up = jnp.pad(rwkv_g_up[0], ((0, 2 * LANES - GATE_RANK), (0, 0))).astype(BF16)
    lane_head = jnp.arange(LANES, dtype=jnp.int32) // A_HEAD_DIM
    ones_bd = (lane_head[:, None] == lane_head[None, :]).astype(BF16)
    rwkv_wts = (mu_rkv, mu_l, rwkv_w0.reshape(1, W), rwkv_a0.reshape(1, W), wa_up, g_up,
                rwkv_k_k.reshape(1, W), rwkv_k_a.reshape(1, W), rwkv_r_k.reshape(1, W),
                rwkv_lnx_w.reshape(1, W), rwkv_lnx_b.reshape(1, W), ones_bd)
    taps = ssm_conv_w[0].T
    cbias = ssm_conv_b.reshape(1, -1)
    src_head = jnp.where(jnp.arange(LANES) < 3 * B_HEADS, jnp.arange(LANES) % B_HEADS, -1)
    e3 = (src_head[:, None] == (jnp.arange(B_WIDTH) // B_HEAD_DIM)[None, :]).astype(BF16)
    mamba_wts = (taps[:, :B_WIDTH], taps[:, B_WIDTH:], cbias[:, :B_WIDTH], cbias[:, B_WIDTH:],
                 _pad_cols(ssm_dt_bias.reshape(1, B_HEADS), DT_PAD),
                 _pad_cols(ssm_A_log.reshape(1, B_HEADS), DT_PAD),
                 jnp.repeat(ssm_D.reshape(1, B_HEADS), B_HEAD_DIM, axis=1),
                 ssm_norm_w.reshape(1, B_WIDTH), e3)
    wo = w_out[0].astype(BF16)
    wo_a, wo_b = wo[:W], wo[W:]
    nfw = norm_ffn_w.reshape(1, D_MODEL)
    wg = ffn_w_gate[0].astype(BF16)
    wu = ffn_w_up[0].astype(BF16)
    wd = ffn_w_down[0].astype(BF16)
    nlw = norm_final_w.reshape(1, D_MODEL)

    tm_small = 384
    n_small = bs * ls + n_meta
    m_small = -(-n_small // tm_small) * tm_small
    x_small = jnp.concatenate([x_sample.reshape(bs * ls, d), meta_tokens,
                               jnp.zeros((m_small - n_small, d), F32)], axis=0)
    x_big = x_prompt.reshape(bp * lp, d)
    tm_big = math.gcd(bp * lp, 512)

    proj_small = _proj(x_small, nmw, w_perm, m_small)
    proj_big = _proj(x_big, nmw, w_perm, math.gcd(bp * lp, 1024))

    zeros = lambda *s: jnp.zeros(s, F32)
    sh = state_rwkv_shift[0]
    smp_rwkv = (sh[:, None, 0:3 * W], _pad_cols(sh[:, None, 3 * W:], LR_PAD), state_rwkv_wkv[0])
    conv0 = jnp.pad(state_ssm_conv[0], ((0, 0), (CONV_ROWS - (CONV_W - 1), 0), (0, 0)))
    smp_mamba = (conv0[..., :B_WIDTH], conv0[..., B_WIDTH:], state_ssm[0])
    meta_rwkv0 = (zeros(1, 1, 3 * W), zeros(1, 1, LR_PAD), zeros(1, A_HEADS, A_HEAD_DIM, A_HEAD_DIM))
    meta_mamba0 = (zeros(1, CONV_ROWS, B_WIDTH), zeros(1, CONV_ROWS, BC_WIDTH),
                   zeros(1, B_HEADS, B_HEAD_DIM, D_STATE))

    ya_m, *meta_rwkv = _rwkv(proj_small, bs * ls, 1, 1, n_meta, *meta_rwkv0, False, rwkv_wts)
    yb_m, *meta_mamba = _mamba(proj_small, bs * ls, 1, 1, n_meta, *meta_mamba0, False, mamba_wts)
    ya_s, *s_rwkv = _rwkv(proj_small, 0, bs, 1, ls, *smp_rwkv, False, rwkv_wts)
    yb_s, *s_mamba = _mamba(proj_small, 0, bs, 1, ls, *smp_mamba, False, mamba_wts)
    ya_p, *p_rwkv = _rwkv(proj_big, 0, bp, lp // CHUNK, CHUNK, *meta_rwkv, True, rwkv_wts)
    yb_p, *p_mamba = _mamba(proj_big, 0, bp, lp // CHUNK, CHUNK, *meta_mamba, True, mamba_wts)

    def tail(ya, yb, x, tm, tm_ffn1):
        x1, h2 = _outproj(x, ya, yb, wo_a, wo_b, nfw, tm)
        g = _ffn1(h2, wg, wu, tm_ffn1, 512)
        return _ffn2(g, wd, x1, nlw, tm, D_FF // 2)

    pad_rows = lambda *parts: jnp.concatenate(
        list(parts) + [jnp.zeros((m_small - n_small, parts[0].shape[1]), parts[0].dtype)], axis=0)
    y_small = tail(pad_rows(ya_s, ya_m), pad_rows(yb_s, yb_m), x_small, tm_small, m_small)
    y_big = tail(ya_p, yb_p, x_big, tm_big, math.gcd(bp * lp, 1024))

    def states_out(rw, mb):
        sh_rkv, sh_l, wkv = rw
        cx, cbc, ssm = mb
        shift = jnp.concatenate([sh_rkv[:, 0], sh_l[:, 0, :LOWRANK]], axis=-1)[None]
        conv = jnp.concatenate([cx, cbc], axis=-1)[None, :, CONV_ROWS - (CONV_W - 1):]
        return shift, wkv[None], conv, ssm[None]

    y_prompt = y_big.reshape(bp, lp, d)
    y_sample = y_small[:bs * ls].reshape(bs, ls, d)
    return (y_prompt, y_sample) + states_out(p_rwkv, p_mamba) + states_out(s_rwkv, s_mamba)
```

```python
import functools
import math

import jax
import jax.numpy as jnp
from jax import lax
from jax.experimental import pallas as pl
from jax.experimental.pallas import tpu as pltpu

F32 = jnp.float32
BF16 = jnp.bfloat16

D_MODEL = 2048
CHUNK = 64
EPS = 1e-6
LNX_EPS = 64e-5
A_WIDTH = 1024
A_HEADS = 16
A_HEAD_DIM = 64
A_PAIRS = A_HEADS // 2
DECAY_RANK = 64
AAA_RANK = 64
GATE_RANK = 160
B_WIDTH = 1024
B_HEADS = 16
B_HEAD_DIM = 64
B_PAIRS = B_HEADS // 2
B_GROUPS = 2
D_STATE = 128
CONV_W = 4
BC_WIDTH = 2 * B_GROUPS * D_STATE
A_COLS = 3 * A_WIDTH + DECAY_RANK + AAA_RANK + GATE_RANK
LOWRANK = DECAY_RANK + AAA_RANK + GATE_RANK
D_FF = 5632

LANES = 128
SUBLANES = 8
VMEM_LIMIT = 56 * 1024 * 1024

LR_PAD = 512
DT_PAD = 128
DT_COLS = 256
COL_R, COL_K, COL_V, COL_Z, COL_X = 0, 1024, 2048, 3072, 4096
COL_BC = 5120
COL_LR = 5632
COL_DT = 6144
PROJ_COLS = COL_DT + DT_COLS
PROJ_TN = 1280
CONV_ROWS = SUBLANES

NT = (((1,), (1,)), ((), ()))
TN = (((0,), (0,)), ((), ()))


def _cparams(sem):
    return pltpu.CompilerParams(dimension_semantics=sem, vmem_limit_bytes=VMEM_LIMIT)


def _dot(x, y):
    return jnp.dot(x, y, preferred_element_type=F32)


def _block_diag_rhs(x, lane, split):
    zero = jnp.zeros_like(x)
    return jnp.concatenate([jnp.where(lane < split, x, zero), jnp.where(lane < split, zero, x)], axis=0)


MAIN_DONE = "main done"


def _sigmoid(t):
    return 0.5 + 0.5 * jnp.tanh(0.5 * t)


def _chunk_pipeline(nc):
    if nc > 1:
        return True, (lambda c: jnp.minimum(c, nc - 1)), (lambda c: jnp.maximum(c - 1, 0))
    return False, (lambda c: c), (lambda c: c)


def _split3(x):
    hi = x.astype(BF16)
    r1 = x - hi.astype(F32)
    mid = r1.astype(BF16)
    return hi, mid, (r1 - mid.astype(F32)).astype(BF16)


def _proj_kernel(x_ref, nw_ref, w_ref, o_ref, h_ref):
    @pl.when(pl.program_id(1) == 0)
    def _():
        x = x_ref[...]
        ms = jnp.mean(x * x, axis=-1, keepdims=True)
        h_ref[...] = (x * lax.rsqrt(ms + EPS) * nw_ref[...]).astype(BF16)

    o_ref[...] = _dot(h_ref[...], w_ref[...])


def _proj(x, norm_w, w, tm):
    m = x.shape[0]
    return pl.pallas_call(
        _proj_kernel,
        grid=(m // tm, PROJ_COLS // PROJ_TN),
        in_specs=[
            pl.BlockSpec((tm, D_MODEL), lambda i, j: (i, 0)),
            pl.BlockSpec((1, D_MODEL), lambda i, j: (0, 0)),
            pl.BlockSpec((D_MODEL, PROJ_TN), lambda i, j: (0, j)),
        ],
        out_specs=pl.BlockSpec((tm, PROJ_TN), lambda i, j: (i, j)),
        out_shape=jax.ShapeDtypeStruct((m, PROJ_COLS), F32),
        scratch_shapes=[pltpu.VMEM((tm, D_MODEL), BF16)],
        compiler_params=_cparams(("parallel", "arbitrary")),
        name="proj",
    )(x, norm_w, w)


def _outproj_kernel(x_ref, ya_ref, yb_ref, wa_ref, wb_ref, nw_ref, x1_ref, h2_ref):
    x1 = x_ref[...] + _dot(ya_ref[...], wa_ref[...]) + _dot(yb_ref[...], wb_ref[...])
    x1_ref[...] = x1
    ms = jnp.mean(x1 * x1, axis=-1, keepdims=True)
    h2_ref[...] = (x1 * lax.rsqrt(ms + EPS) * nw_ref[...]).astype(BF16)


def _outproj(x, ya, yb, wa, wb, norm_w, tm):
    m = x.shape[0]
    row = lambda i: (i, 0)
    fixed = lambda i: (0, 0)
    return pl.pallas_call(
        _outproj_kernel,
        grid=(m // tm,),
        in_specs=[
            pl.BlockSpec((tm, D_MODEL), row),
            pl.BlockSpec((tm, A_WIDTH), row),
            pl.BlockSpec((tm, B_WIDTH), row),
            pl.BlockSpec((A_WIDTH, D_MODEL), fixed),
            pl.BlockSpec((B_WIDTH, D_MODEL), fixed),
            pl.BlockSpec((1, D_MODEL), fixed),
        ],
        out_specs=[pl.BlockSpec((tm, D_MODEL), row), pl.BlockSpec((tm, D_MODEL), row)],
        out_shape=[jax.ShapeDtypeStruct((m, D_MODEL), F32),
                   jax.ShapeDtypeStruct((m, D_MODEL), BF16)],
        compiler_params=_cparams(("parallel",)),
        name="outproj",
    )(x, ya, yb, wa, wb, norm_w)


def _ffn1_kernel(h_ref, wg_ref, wu_ref, o_ref):
    h = h_ref[...]
    g = _dot(h, wg_ref[...])
    u = _dot(h, wu_ref[...])
    o_ref[...] = (g * jax.nn.sigmoid(g) * u).astype(BF16)


def _ffn1(h2, wg, wu, tm, tn):
    m = h2.shape[0]
    return pl.pallas_call(
        _ffn1_kernel,
        grid=(m // tm, D_FF // tn),
        in_specs=[
            pl.BlockSpec((tm, D_MODEL), lambda i, j: (i, 0)),
            pl.BlockSpec((D_MODEL, tn), lambda i, j: (0, j)),
            pl.BlockSpec((D_MODEL, tn), lambda i, j: (0, j)),
        ],
        out_specs=pl.BlockSpec((tm, tn), lambda i, j: (i, j)),
        out_shape=jax.ShapeDtypeStruct((m, D_FF), BF16),
        compiler_params=_cparams(("parallel", "parallel")),
        name="ffn1",
    )(h2, wg, wu)


def _ffn2_kernel(g_ref, wd_ref, x1_ref, nw_ref, o_ref):
    k = pl.program_id(1)

    @pl.when(k == 0)
    def _():
        o_ref[...] = x1_ref[...] + _dot(g_ref[...], wd_ref[...])

    @pl.when(jnp.logical_and(k > 0, k < pl.num_programs(1) - 1))
    def _():
        o_ref[...] += _dot(g_ref[...], wd_ref[...])

    @pl.when(k == pl.num_programs(1) - 1)
    def _():
        x2 = o_ref[...] + _dot(g_ref[...], wd_ref[...])
        ms = jnp.mean(x2 * x2, axis=-1, keepdims=True)
        o_ref[...] = x2 * lax.rsqrt(ms + EPS) * nw_ref[...]


def _ffn2(g, wd, x1, norm_w, tm, tk):
    m = g.shape[0]
    assert D_FF // tk >= 2
    return pl.pallas_call(
        _ffn2_kernel,
        grid=(m // tm, D_FF // tk),
        in_specs=[
            pl.BlockSpec((tm, tk), lambda i, k: (i, k)),
            pl.BlockSpec((tk, D_MODEL), lambda i, k: (k, 0)),
            pl.BlockSpec((tm, D_MODEL), lambda i, k: (i, 0)),
            pl.BlockSpec((1, D_MODEL), lambda i, k: (0, 0)),
        ],
        out_specs=pl.BlockSpec((tm, D_MODEL), lambda i, k: (i, 0)),
        out_shape=jax.ShapeDtypeStruct((m, D_MODEL), F32),
        compiler_params=_cparams(("parallel", "arbitrary")),
        name="ffn2",
    )(g, wd, x1, norm_w)


def _shift_rows(z, prev_row):
    rolled = pltpu.roll(z, 1, axis=0)
    row = lax.broadcasted_iota(jnp.int32, z.shape, 0)
    return jnp.where(row == 0, prev_row, rolled)


def _rwkv_program(zr_ref, zk_ref, zv_ref, zl_ref, sh_rkv_ref, sh_l_ref, s0_ref,
                 mu_rkv_ref, mu_l_ref, w0_ref, a0_ref, wa_up_ref, g_up_ref,
                 kk_ref, ka_ref, rk_ref, lnw_ref, lnb_ref, ones_ref,
                 ya_ref, osh_rkv_ref, osh_l_ref, os_ref,
                 prev_ref, s_ref, ops_ref, aux_ref, plast_ref, *, chunk, pipelined):
    C = chunk
    W = A_WIDTH
    HD = A_HEAD_DIM
    c_idx = pl.program_id(1)
    wslot = c_idx % 2 if pipelined else 0
    rslot = 1 - wslot if pipelined else 0
    PR = range(A_PAIRS)
    cols = lambda t, p: t[:, p * LANES:(p + 1) * LANES]
    OP_AT, OP_RT, OP_KH, OP_BH, OP_KE, OP_BE, OP_V = range(7)
    AUX_BONUS, AUX_GATE = range(2)

    @pl.when(c_idx == 0)
    def _():
        prev_ref[:, 0:3 * W] = sh_rkv_ref[0]
        prev_ref[:, 3 * W:3 * W + LR_PAD] = sh_l_ref[0]
        zero = jnp.zeros((HD, HD), F32)
        for p in PR:
            s_ref[p] = jnp.concatenate(
                [jnp.concatenate([s0_ref[0, 2 * p], zero], axis=1),
                 jnp.concatenate([zero, s0_ref[0, 2 * p + 1]], axis=1)], axis=0)
        if pipelined:
            ops_ref[1] = jnp.zeros(ops_ref.shape[1:], BF16)
            aux_ref[1] = jnp.zeros(aux_ref.shape[1:], F32)
            plast_ref[1] = jnp.ones(plast_ref.shape[1:], F32)

    yield
    ti2 = lax.broadcasted_iota(jnp.int32, (C, 2 * C), 0)
    lane_tt = lax.broadcasted_iota(jnp.int32, (C, 2 * C), 1)
    sj2 = jnp.where(lane_tt >= C, lane_tt - C, lane_tt)
    incl2 = sj2 <= ti2
    strict2 = sj2 < ti2
    tri2_b = jnp.where(incl2, 1.0, 0.0).astype(BF16)
    lane_c = lax.broadcasted_iota(jnp.int32, (C, LANES), 1)
    bd_i = lax.broadcasted_iota(jnp.int32, (LANES, LANES), 0) // HD
    bd_j = lax.broadcasted_iota(jnp.int32, (LANES, LANES), 1) // HD
    bd_mask = bd_i == bd_j
    n_sq = int(math.log2(C)) - 1
    bd_ch = lambda x: _block_diag_rhs(x, lane_c, HD)
    bd_tt = lambda x: _block_diag_rhs(x, lane_tt, C)

    def mixed(z, prev_lo, mu):
        width = z.shape[1]
        zp = _shift_rows(z, prev_ref[:, prev_lo:prev_lo + width])
        prev_ref[:, prev_lo:prev_lo + width] = z[C - 1:C, :]
        return z + mu * (zp - z)

    zl_raw = zl_ref[...]
    zl = mixed(zl_raw, 3 * W, mu_l_ref[...])
    osh_l_ref[0] = zl_raw[C - 1:C, :]
    l1 = zl[:, 0:LANES]
    l1 = jnp.where(lane_c < DECAY_RANK, jnp.tanh(l1), l1)
    wa = _dot(l1.astype(BF16), wa_up_ref[...])
    gl = _sigmoid(zl[:, LANES:3 * LANES])
    aux_ref[wslot, AUX_GATE] = _dot(gl.astype(BF16), g_up_ref[...])

    def hsum_tiles(tiles):
        s = _dot(jnp.concatenate(tiles, axis=0).astype(BF16), ones_ref[...])
        return [s[i * C:(i + 1) * C] for i in range(len(tiles))]

    def prepare_tile(p):
        lo, hi = p * LANES, (p + 1) * LANES
        sl = slice(lo, hi)

        def mixed_rkv(z_ref, base):
            z = z_ref[:, sl]
            osh_rkv_ref[0, :, base + lo:base + hi] = z[C - 1:C, :]
            return mixed(z, base + lo, mu_rkv_ref[:, base + lo:base + hi])

        hsum = lambda x: hsum_tiles([x])[0]
        r = mixed_rkv(zr_ref, 0)
        k = mixed_rkv(zk_ref, W)
        v = mixed_rkv(zv_ref, 2 * W)
        wlog = -jax.nn.softplus(-(w0_ref[:, sl] + wa[:, sl])) - 0.5
        lw = -jnp.exp(wlog)
        a = _sigmoid(a0_ref[:, sl] + wa[:, W + lo:W + hi])
        kk = k * kk_ref[:, sl]
        k2 = k * (1.0 + (a - 1.0) * ka_ref[:, sl])
        kk = kk * lax.rsqrt(jnp.maximum(hsum(kk * kk), 1e-24))
        ka = kk * a
        lw_hi = lw.astype(BF16)
        lw_lo = (lw - lw_hi.astype(F32)).astype(BF16)
        cum = _dot(tri2_b, jnp.concatenate([lw_hi, lw_lo], axis=0))
        cum_last = cum[C - 1:C, :]
        e_neg = jnp.exp(-cum)
        e_end = jnp.exp(cum_last - cum)
        ops_ref[wslot, OP_AT, :, sl] = (kk * jnp.exp(cum - lw)).astype(BF16)
        ops_ref[wslot, OP_RT, :, sl] = (r * jnp.exp(cum)).astype(BF16)
        ops_ref[wslot, OP_KH, :, sl] = (k2 * e_neg).astype(BF16)
        ops_ref[wslot, OP_BH, :, sl] = (ka * e_neg).astype(BF16)
        ops_ref[wslot, OP_KE, :, sl] = (k2 * e_end).astype(BF16)
        ops_ref[wslot, OP_BE, :, sl] = (-ka * e_end).astype(BF16)
        ops_ref[wslot, OP_V, :, sl] = v.astype(BF16)
        plast_ref[wslot, :, sl] = jnp.exp(cum_last)
        aux_ref[wslot, AUX_BONUS, :, sl] = hsum(r * k2 * rk_ref[:, sl]) * v

    pending = list(PR)

    def prepare_next():
        if pending:
            prepare_tile(pending.pop(0))

    if not pipelined:
        while pending:
            prepare_next()

    op = lambda i, p: ops_ref[rslot, i, :, p * LANES:(p + 1) * LANES]
    s_old = [s_ref[p] for p in PR]
    v_b = [op(OP_V, p) for p in PR]
    lhs = [jnp.concatenate([op(OP_AT, p), op(OP_RT, p)], axis=0) for p in PR]
    rhs = [jnp.concatenate([bd_ch(op(OP_KH, p)), bd_ch(op(OP_BH, p))], axis=0) for p in PR]
    upd_rhs = [jnp.concatenate([op(OP_KE, p), op(OP_BE, p)], axis=0) for p in PR]
    p_last = plast_ref[rslot]
    m1 = [lax.dot_general(lhs[p], rhs[p], NT, preferred_element_type=F32) for p in PR]
    m2 = [lax.dot_general(lhs[p], s_old[p].astype(BF16), NT, preferred_element_type=F32) for p in PR]
    prepare_next()
    yield
    a_k = [jnp.where(strict2, m1[p][0:C, 0:2 * C], 0.0).astype(BF16) for p in PR]
    a_b = [jnp.where(strict2, m1[p][0:C, 2 * C:4 * C], 0.0) for p in PR]
    b_kb = [jnp.concatenate([jnp.where(incl2, m1[p][C:2 * C, 0:2 * C], 0.0),
                             jnp.where(incl2, -m1[p][C:2 * C, 2 * C:4 * C], 0.0)], axis=1).astype(BF16)
            for p in PR]
    bd_v = [bd_ch(v_b[p]) for p in PR]
    u = [m2[p][0:C] + _dot(a_k[p], bd_v[p]) for p in PR]
    prepare_next()
    yield
    pw_b = [a_b[p].astype(BF16) for p in PR]
    u = [u[p] - _dot(pw_b[p], bd_ch(u[p].astype(BF16))) for p in PR]
    prepare_next()
    yield
    for _ in range(n_sq):
        pw_b = [_dot(pw_b[p], bd_tt(pw_b[p])).astype(BF16) for p in PR]
        u = [u[p] + _dot(pw_b[p], bd_ch(u[p].astype(BF16))) for p in PR]
        prepare_next()
        yield
    u_b = [u[p].astype(BF16) for p in PR]
    ys = [m2[p][C:2 * C] + _dot(b_kb[p], jnp.concatenate([bd_v[p], bd_ch(u_b[p])], axis=0)) for p in PR]
    upd = [lax.dot_general(jnp.concatenate([v_b[p], u_b[p]], axis=0), upd_rhs[p],
                           TN, preferred_element_type=F32) for p in PR]
    for p in PR:
        s_ref[p] = s_old[p] * cols(p_last, p) + jnp.where(bd_mask, upd[p], 0.0)
    while pending:
        prepare_next()
    yield

    inv_n = 1.0 / A_HEAD_DIM
    mean = hsum_tiles(ys)
    d = [ys[p] - mean[p] * inv_n for p in PR]
    var = hsum_tiles([d[p] * d[p] for p in PR])
    yn = jnp.concatenate([d[p] * lax.rsqrt(var[p] * inv_n + LNX_EPS) for p in PR], axis=1)
    y = yn * lnw_ref[...] + lnb_ref[...] + aux_ref[rslot, AUX_BONUS]
    ya_ref[...] = (y * aux_ref[rslot, AUX_GATE]).astype(BF16)
    yield MAIN_DONE

    @pl.when(c_idx == pl.num_programs(1) - 1)
    def _():
        for p in PR:
            s_pair = s_ref[p]
            os_ref[0, 2 * p] = s_pair[0:HD, 0:HD]
            os_ref[0, 2 * p + 1] = s_pair[HD:2 * HD, HD:2 * HD]


def _mamba_program(z_ref, x_ref, bc_ref, dt_ref, cx0_ref, cbc0_ref, s0_ref,
                  cwx_ref, cwbc_ref, cbx_ref, cbbc_ref, dtb_ref, alog_ref, dsk_ref, nw_ref, e3_ref,
                  yb_ref, ocx_ref, ocbc_ref, os_ref,
                  cx_ref, cbc_ref, s_ref, f32_ref, b16_ref, bcs_ref, sdec_ref, *, chunk, pipelined):
    C = chunk
    c_idx = pl.program_id(1)
    wslot = c_idx % 2 if pipelined else 0
    rslot = 1 - wslot if pipelined else 0
    GW = B_WIDTH // B_GROUPS
    HD = B_HEAD_DIM
    PR = range(B_PAIRS)
    cols = lambda t, p: t[:, p * LANES:(p + 1) * LANES]
    F_SKIP, F_GATE, F_ACS = range(3)
    B_XDT, B_XEND = range(2)

    @pl.when(c_idx == 0)
    def _():
        cx_ref[0:CONV_ROWS, :] = cx0_ref[0]
        cbc_ref[0:CONV_ROWS, :] = cbc0_ref[0]
        for p in PR:
            s_ref[:, p * LANES:(p + 1) * LANES] = jnp.concatenate(
                [s0_ref[0, 2 * p].T, s0_ref[0, 2 * p + 1].T], axis=1)
        if pipelined:
            f32_ref[1] = jnp.zeros(f32_ref.shape[1:], F32)
            b16_ref[1] = jnp.zeros(b16_ref.shape[1:], BF16)
            bcs_ref[1] = jnp.zeros(bcs_ref.shape[1:], BF16)
            sdec_ref[1] = jnp.ones(sdec_ref.shape[1:], F32)

    yield
    silu = lambda t: t * _sigmoid(t)

    def causal_conv(raw, ext_ref, taps_ref, bias_ref, sl):
        ext_ref[CONV_ROWS:CONV_ROWS + C, sl] = raw
        acc = bias_ref[:, sl] + taps_ref[CONV_W - 1:CONV_W, sl] * raw
        for j in range(CONV_W - 1):
            start = CONV_ROWS - (CONV_W - 1 - j)
            acc = acc + taps_ref[j:j + 1, sl] * ext_ref[start:start + C, sl]
        ext_ref[0:CONV_ROWS, sl] = raw[C - CONV_ROWS:C, :]
        return silu(acc)

    lane_c = lax.broadcasted_iota(jnp.int32, (C, LANES), 1)
    valid = lane_c < B_HEADS
    dt = jnp.where(valid, jax.nn.softplus(dt_ref[...] + dtb_ref[...]), 0.0)
    d_a = dt * (-jnp.exp(alog_ref[...]))

    ti3 = lax.broadcasted_iota(jnp.int32, (C, 3 * C), 0)
    sj3 = lax.broadcasted_iota(jnp.int32, (C, 3 * C), 1)
    sj3 = jnp.where(sj3 >= 2 * C, sj3 - 2 * C, jnp.where(sj3 >= C, sj3 - C, sj3))
    tri3 = jnp.where(sj3 <= ti3, 1.0, 0.0).astype(BF16)
    a_cs = _dot(tri3, jnp.concatenate(_split3(d_a), axis=0))

    def pack3(t):
        hi, mid, lo = _split3(t)
        return (hi.astype(F32) + pltpu.roll(mid.astype(F32), B_HEADS, axis=1)
                + pltpu.roll(lo.astype(F32), 2 * B_HEADS, axis=1)).astype(BF16)

    expanded = _dot(jnp.concatenate([pack3(dt), pack3(a_cs)], axis=0), e3_ref[...])
    dt_e = expanded[0:C]
    acs_new = expanded[C:2 * C]
    alast_new = acs_new[C - 1:C, :]
    f32_ref[wslot, F_ACS] = acs_new
    sdec_ref[wslot] = jnp.exp(alast_new)

    def prepare_bc():
        bcs_ref[wslot] = causal_conv(bc_ref[...], cbc_ref, cwbc_ref, cbbc_ref,
                                     slice(0, BC_WIDTH)).astype(BF16)
        ocbc_ref[0] = cbc_ref[0:CONV_ROWS, :]

    def prepare_tile(t):
        sl = slice(t * LANES, (t + 1) * LANES)
        xs = causal_conv(x_ref[:, sl], cx_ref, cwx_ref, cbx_ref, sl)
        ocx_ref[0, :, sl] = cx_ref[0:CONV_ROWS, sl]
        xdt = xs * dt_e[:, sl]
        b16_ref[wslot, B_XDT, :, sl] = xdt.astype(BF16)
        b16_ref[wslot, B_XEND, :, sl] = (xdt * jnp.exp(alast_new[:, sl] - acs_new[:, sl])).astype(BF16)
        f32_ref[wslot, F_SKIP, :, sl] = xs * dsk_ref[:, sl]
        f32_ref[wslot, F_GATE, :, sl] = silu(z_ref[:, sl])

    pending = [prepare_bc] + [functools.partial(prepare_tile, t) for t in PR]

    def prepare_next():
        if pending:
            pending.pop(0)()

    if not pipelined:
        while pending:
            prepare_next()
            yield
    yield

    ti2 = lax.broadcasted_iota(jnp.int32, (C, 2 * C), 0)
    lane_tt = lax.broadcasted_iota(jnp.int32, (C, 2 * C), 1)
    sj2 = jnp.where(lane_tt >= C, lane_tt - C, lane_tt)
    causal2 = sj2 <= ti2
    eye2 = sj2 == ti2

    acs_e = f32_ref[rslot, F_ACS]
    s_decay = sdec_ref[rslot]
    bcm = bcs_ref[rslot]
    cb2, y_state = [], []
    for g in range(B_GROUPS):
        bm = bcm[:, g * D_STATE:(g + 1) * D_STATE]
        cm = bcm[:, (B_GROUPS + g) * D_STATE:(B_GROUPS + g + 1) * D_STATE]
        cb = lax.dot_general(cm, bm, NT, preferred_element_type=F32)
        cb2.append(jnp.concatenate([cb, cb], axis=1))
        gsl = slice(g * GW, (g + 1) * GW)
        s_old = s_ref[:, gsl]
        y_state.append(_dot(cm, s_old.astype(BF16)) * jnp.exp(acs_e[:, gsl]))
        s_ref[:, gsl] = s_old * s_decay[:, gsl] + lax.dot_general(
            bm, b16_ref[rslot, B_XEND, :, gsl], TN, preferred_element_type=F32)
        prepare_next()
        yield
    y_state = jnp.concatenate(y_state, axis=1)

    ys = []
    for p in PR:
        if C == HD:
            col_form = cols(acs_e, p)
        else:
            col_form = jnp.concatenate([acs_e[:, p * LANES:p * LANES + C],
                                        acs_e[:, p * LANES + HD:p * LANES + HD + C]], axis=1)
        row_form = jnp.sum(jnp.where(eye2, col_form, 0.0), axis=0, keepdims=True)
        m = jnp.where(causal2, jnp.exp(col_form - row_form), 0.0) * cb2[p // (B_PAIRS // B_GROUPS)]
        xdt_p = b16_ref[rslot, B_XDT, :, p * LANES:(p + 1) * LANES]
        ys.append(_dot(m.astype(BF16), _block_diag_rhs(xdt_p, lane_c, HD)))
        prepare_next()
        yield
    while pending:
        prepare_next()
    y = jnp.concatenate(ys, axis=1) + y_state

    y = (y + f32_ref[rslot, F_SKIP]) * f32_ref[rslot, F_GATE]
    outs = []
    for g in range(B_GROUPS):
        yg = y[:, g * GW:(g + 1) * GW]
        ms = jnp.mean(yg * yg, axis=-1, keepdims=True)
        outs.append(yg * lax.rsqrt(ms + EPS))
    yb_ref[...] = (jnp.concatenate(outs, axis=1) * nw_ref[...]).astype(BF16)
    yield MAIN_DONE

    @pl.when(c_idx == pl.num_programs(1) - 1)
    def _():
        for h in range(B_HEADS):
            os_ref[0, h] = s_ref[:, h * HD:(h + 1) * HD].T


N_MIXER_OUT = 4


def _mixer_kernel(*refs, chunk, pipelined, n_rwkv_in, n_mamba_in, n_rwkv_scratch):
    refs = list(refs)
    n_in = n_rwkv_in + n_mamba_in
    outs = refs[n_in:n_in + 2 * N_MIXER_OUT]
    scratch = refs[n_in + 2 * N_MIXER_OUT:]
    programs = [
        _rwkv_program(*refs[:n_rwkv_in], *outs[:N_MIXER_OUT], *scratch[:n_rwkv_scratch],
                      chunk=chunk, pipelined=pipelined),
        _mamba_program(*refs[n_rwkv_in:n_in], *outs[N_MIXER_OUT:], *scratch[n_rwkv_scratch:],
                       chunk=chunk, pipelined=pipelined),
    ]
    main_done = [False] * len(programs)
    while not all(main_done):
        for i, prog in enumerate(programs):
            if not main_done[i]:
                main_done[i] = next(prog) is MAIN_DONE
    for prog in programs:
        for _ in prog:
            pass


def _mixers(proj, row_off, nb, nc, chunk, rwkv_state, mamba_state, shared_state, rwkv_wts, mamba_wts):
    C = chunk
    assert CONV_ROWS <= C <= B_HEAD_DIM
    rb = row_off // C
    W = A_WIDTH
    pipelined, in_chunk, out_chunk = _chunk_pipeline(nc)
    slots = 2 if pipelined else 1
    zspec = lambda width, col: pl.BlockSpec(
        (C, width), lambda b, c: (rb + b * nc + in_chunk(c), col // width))
    st = (lambda b: 0) if shared_state else (lambda b: b)
    full = lambda arr: pl.BlockSpec(arr.shape, lambda b, c: (0, 0))
    yspec = pl.BlockSpec((C, W), lambda b, c: (b * nc + out_chunk(c), 0))
    y_shape = jax.ShapeDtypeStruct((nb * nc * C, W), BF16)

    def state_specs(blocks):
        zeros = lambda blk: (0,) * (len(blk) - 1)
        ins = [pl.BlockSpec(blk, lambda b, c, z=zeros(blk): (st(b),) + z) for blk in blocks]
        outs = [pl.BlockSpec(blk, lambda b, c, z=zeros(blk): (b,) + z) for blk in blocks]
        shapes = [jax.ShapeDtypeStruct((nb,) + blk[1:], F32) for blk in blocks]
        return ins, outs, shapes

    r_in, r_out, r_shapes = state_specs(
        [(1, 1, 3 * W), (1, 1, LR_PAD), (1, A_HEADS, A_HEAD_DIM, A_HEAD_DIM)])
    m_in, m_out, m_shapes = state_specs(
        [(1, CONV_ROWS, B_WIDTH), (1, CONV_ROWS, BC_WIDTH), (1, B_HEADS, B_HEAD_DIM, D_STATE)])
    rwkv_in = [zspec(W, COL_R), zspec(W, COL_K), zspec(W, COL_V), zspec(LR_PAD, COL_LR)] + r_in \
        + [full(w) for w in rwkv_wts]
    mamba_in = [zspec(B_WIDTH, COL_Z), zspec(B_WIDTH, COL_X), zspec(BC_WIDTH, COL_BC),
                zspec(DT_PAD, COL_DT)] + m_in + [full(w) for w in mamba_wts]
    rwkv_scratch = [
        pltpu.VMEM((1, 3 * W + LR_PAD), F32),
        pltpu.VMEM((A_PAIRS, LANES, LANES), F32),
        pltpu.VMEM((slots, 7, C, W), BF16),
        pltpu.VMEM((slots, 2, C, W), F32),
        pltpu.VMEM((slots, 1, W), F32),
    ]
    mamba_scratch = [
        pltpu.VMEM((CONV_ROWS + C, B_WIDTH), F32),
        pltpu.VMEM((CONV_ROWS + C, BC_WIDTH), F32),
        pltpu.VMEM((D_STATE, B_WIDTH), F32),
        pltpu.VMEM((slots, 3, C, B_WIDTH), F32),
        pltpu.VMEM((slots, 2, C, B_WIDTH), BF16),
        pltpu.VMEM((slots, C, BC_WIDTH), BF16),
        pltpu.VMEM((slots, 1, B_WIDTH), F32),
    ]
    outs = pl.pallas_call(
        functools.partial(_mixer_kernel, chunk=C, pipelined=pipelined, n_rwkv_in=len(rwkv_in),
                          n_mamba_in=len(mamba_in), n_rwkv_scratch=len(rwkv_scratch)),
        grid=(nb, nc + 1 if pipelined else nc),
        in_specs=rwkv_in + mamba_in,
        out_specs=[yspec] + r_out + [yspec] + m_out,
        out_shape=[y_shape] + r_shapes + [y_shape] + m_shapes,
        scratch_shapes=rwkv_scratch + mamba_scratch,
        compiler_params=_cparams(("parallel", "arbitrary")),
        name="mixers",
    )(proj, proj, proj, proj, *rwkv_state, *rwkv_wts, proj, proj, proj, proj, *mamba_state, *mamba_wts)
    return outs[:N_MIXER_OUT], outs[N_MIXER_OUT:]


def _pad_cols(a, width):
    return jnp.pad(a, [(0, 0)] * (a.ndim - 1) + [(0, width - a.shape[-1])])


def kernel(x_prompt, x_sample, state_rwkv_shift, state_rwkv_wkv, state_ssm_conv, state_ssm, meta_tokens, norm_mix_w, w_in, rwkv_mu, rwkv_w0, rwkv_w_up, rwkv_a0, rwkv_a_up, rwkv_g_up, rwkv_k_k, rwkv_k_a, rwkv_r_k, rwkv_lnx_w, rwkv_lnx_b, ssm_conv_w, ssm_conv_b, ssm_dt_bias, ssm_A_log, ssm_D, ssm_norm_w, w_out, norm_ffn_w, ffn_w_gate, ffn_w_up, ffn_w_down, norm_final_w):
    depth = w_in.shape[0]
    assert depth == 1, "single-layer step"
    bp, lp, d = x_prompt.shape
    bs, ls, _ = x_sample.shape
    n_meta = meta_tokens.shape[0]
    assert d == D_MODEL and lp % CHUNK == 0 and ls <= CHUNK and n_meta <= CHUNK
    assert w_in.shape[-1] == A_COLS + 2 * B_WIDTH + BC_WIDTH + B_HEADS
    W = A_WIDTH

    wi = w_in[0]
    w_perm = jnp.concatenate([
        wi[:, 0:3 * W],
        wi[:, A_COLS:A_COLS + 2 * B_WIDTH + BC_WIDTH],
        _pad_cols(wi[:, 3 * W:A_COLS], LR_PAD),
        _pad_cols(wi[:, A_COLS + 2 * B_WIDTH + BC_WIDTH:], DT_COLS),
    ], axis=1).astype(BF16)
    nmw = norm_mix_w.reshape(1, D_MODEL)
    mu = rwkv_mu[0]
    mu_rkv = mu[None, 0:3 * W]
    mu_l = _pad_cols(mu[None, 3 * W:], LR_PAD)
    wa_up = jnp.zeros((LANES, 2 * W), F32)
    wa_up = wa_up.at[0:DECAY_RANK, 0:W].set(rwkv_w_up[0])
    wa_up = wa_up.at[DECAY_RANK:DECAY_RANK + AAA_RANK, W:].set(rwkv_a_up[0]).astype(BF16)
    g_up = jnp.pad(rwkv_g_up[0], ((0, 2 * LANES - GATE_RANK), (0, 0))).astype(BF16)
    lane_head = jnp.arange(LANES, dtype=jnp.int32) // A_HEAD_DIM
    ones_bd = (lane_head[:, None] == lane_head[None, :]).astype(BF16)
    rwkv_wts = (mu_rkv, mu_l, rwkv_w0.reshape(1, W), rwkv_a0.reshape(1, W), wa_up, g_up,
                rwkv_k_k.reshape(1, W), rwkv_k_a.reshape(1, W), rwkv_r_k.reshape(1, W),
                rwkv_lnx_w.reshape(1, W), rwkv_lnx_b.reshape(1, W), ones_bd)
    taps = ssm_conv_w[0].T
    cbias = ssm_conv_b.reshape(1, -1)
    src_head = jnp.where(jnp.arange(LANES) < 3 * B_HEADS, jnp.arange(LANES) % B_HEADS, -1)
    e3 = (src_head[:, None] == (jnp.arange(B_WIDTH) // B_HEAD_DIM)[None, :]).astype(BF16)
    mamba_wts = (taps[:, :B_WIDTH], taps[:, B_WIDTH:], cbias[:, :B_WIDTH], cbias[:, B_WIDTH:],
                 _pad_cols(ssm_dt_bias.reshape(1, B_HEADS), DT_PAD),
                 _pad_cols(ssm_A_log.reshape(1, B_HEADS), DT_PAD),
                 jnp.repeat(ssm_D.reshape(1, B_HEADS), B_HEAD_DIM, axis=1),
                 ssm_norm_w.reshape(1, B_WIDTH), e3)
    wo = w_out[0].astype(BF16)
    wo_a, wo_b = wo[:W], wo[W:]
    nfw = norm_ffn_w.reshape(1, D_MODEL)
    wg = ffn_w_gate[0].astype(BF16)
    wu = ffn_w_up[0].astype(BF16)
    wd = ffn_w_down[0].astype(BF16)
    nlw = norm_final_w.reshape(1, D_MODEL)

    tm_small = 384
    n_small = bs * ls + n_meta
    m_small = -(-n_small // tm_small) * tm_small
    x_small = jnp.concatenate([x_sample.reshape(bs * ls, d), meta_tokens,
                               jnp.zeros((m_small - n_small, d), F32)], axis=0)
    x_big = x_prompt.reshape(bp * lp, d)
    tm_big = math.gcd(bp * lp, 512)

    proj_small = _proj(x_small, nmw, w_perm, m_small)
    proj_big = _proj(x_big, nmw, w_perm, math.gcd(bp * lp, 1024))

    zeros = lambda *s: jnp.zeros(s, F32)
    sh = state_rwkv_shift[0]
    smp_rwkv = (sh[:, None, 0:3 * W], _pad_cols(sh[:, None, 3 * W:], LR_PAD), state_rwkv_wkv[0])
    conv0 = jnp.pad(state_ssm_conv[0], ((0, 0), (CONV_ROWS - (CONV_W - 1), 0), (0, 0)))
    smp_mamba = (conv0[..., :B_WIDTH], conv0[..., B_WIDTH:], state_ssm[0])
    meta_rwkv0 = (zeros(1, 1, 3 * W), zeros(1, 1, LR_PAD), zeros(1, A_HEADS, A_HEAD_DIM, A_HEAD_DIM))
    meta_mamba0 = (zeros(1, CONV_ROWS, B_WIDTH), zeros(1, CONV_ROWS, BC_WIDTH),
                   zeros(1, B_HEADS, B_HEAD_DIM, D_STATE))

    (ya_m, *meta_rwkv), (yb_m, *meta_mamba) = _mixers(
        proj_small, bs * ls, 1, 1, n_meta, meta_rwkv0, meta_mamba0, False, rwkv_wts, mamba_wts)
    (ya_s, *s_rwkv), (yb_s, *s_mamba) = _mixers(
        proj_small, 0, bs, 1, ls, smp_rwkv, smp_mamba, False, rwkv_wts, mamba_wts)
    (ya_p, *p_rwkv), (yb_p, *p_mamba) = _mixers(
        proj_big, 0, bp, lp // CHUNK, CHUNK, meta_rwkv, meta_mamba, True, rwkv_wts, mamba_wts)

    def tail(ya, yb, x, tm, tm_ffn1):
        x1, h2 = _outproj(x, ya, yb, wo_a, wo_b, nfw, tm)
        g = _ffn1(h2, wg, wu, tm_ffn1, 512)
        return _ffn2(g, wd, x1, nlw, tm, D_FF // 2)

    pad_rows = lambda *parts: jnp.concatenate(
        list(parts) + [jnp.zeros((m_small - n_small, parts[0].shape[1]), parts[0].dtype)], axis=0)
    y_small = tail(pad_rows(ya_s, ya_m), pad_rows(yb_s, yb_m), x_small, tm_small, m_small)
    y_big = tail(ya_p, yb_p, x_big, tm_big, math.gcd(bp * lp, 1024))

    def states_out(rw, mb):
        sh_rkv, sh_l, wkv = rw
        cx, cbc, ssm = mb
        shift = jnp.concatenate([sh_rkv[:, 0], sh_l[:, 0, :LOWRANK]], axis=-1)[None]
        conv = jnp.concatenate([cx, cbc], axis=-1)[None, :, CONV_ROWS - (CONV_W - 1):]
        return shift, wkv[None], conv, ssm[None]

    y_prompt = y_big.reshape(bp, lp, d)
    y_sample = y_small[:bs * ls].reshape(bs, ls, d)
    return (y_prompt, y_sample) + states_out(p_rwkv, p_mamba) + states_out(s_rwkv, s_mamba)
```

```python
import functools
import math

import jax
import jax.numpy as jnp
from jax import lax
from jax.experimental import pallas as pl
from jax.experimental.pallas import tpu as pltpu

F32 = jnp.float32
BF16 = jnp.bfloat16

D_MODEL = 2048
CHUNK = 64
EPS = 1e-6
LNX_EPS = 64e-5
A_WIDTH = 1024
A_HEADS = 16
A_HEAD_DIM = 64
A_PAIRS = A_HEADS // 2
DECAY_RANK = 64
AAA_RANK = 64
GATE_RANK = 160
B_WIDTH = 1024
B_HEADS = 16
B_HEAD_DIM = 64
B_PAIRS = B_HEADS // 2
B_GROUPS = 2
D_STATE = 128
CONV_W = 4
BC_WIDTH = 2 * B_GROUPS * D_STATE
A_COLS = 3 * A_WIDTH + DECAY_RANK + AAA_RANK + GATE_RANK
LOWRANK = DECAY_RANK + AAA_RANK + GATE_RANK
D_FF = 5632

LANES = 128
SUBLANES = 8
VMEM_LIMIT = 56 * 1024 * 1024

LR_PAD = 512
DT_PAD = 128
DT_COLS = 256
COL_R, COL_K, COL_V, COL_Z, COL_X = 0, 1024, 2048, 3072, 4096
COL_BC = 5120
COL_LR = 5632
COL_DT = 6144
PROJ_COLS = COL_DT + DT_COLS
PROJ_TN = 1280
CONV_ROWS = SUBLANES

NT = (((1,), (1,)), ((), ()))
TN = (((0,), (0,)), ((), ()))


def _cparams(sem):
    return pltpu.CompilerParams(dimension_semantics=sem, vmem_limit_bytes=VMEM_LIMIT)


def _dot(x, y):
    return jnp.dot(x, y, preferred_element_type=F32)


def _block_diag_rhs(x, lane, split):
    zero = jnp.zeros_like(x)
    return jnp.concatenate([jnp.where(lane < split, x, zero), jnp.where(lane < split, zero, x)], axis=0)


MAIN_DONE = "main done"


def _sigmoid(t):
    return 0.5 + 0.5 * jnp.tanh(0.5 * t)


def _chunk_pipeline(nc):
    if nc > 1:
        return True, (lambda c: jnp.minimum(c, nc - 1)), (lambda c: jnp.maximum(c - 1, 0))
    return False, (lambda c: c), (lambda c: c)


def _split3(x):
    hi = x.astype(BF16)
    r1 = x - hi.astype(F32)
    mid = r1.astype(BF16)
    return hi, mid, (r1 - mid.astype(F32)).astype(BF16)


def _proj_kernel(x_ref, nw_ref, w_ref, o_ref, h_ref):
    @pl.when(pl.program_id(1) == 0)
    def _():
        x = x_ref[...]
        ms = jnp.mean(x * x, axis=-1, keepdims=True)
        h_ref[...] = (x * lax.rsqrt(ms + EPS) * nw_ref[...]).astype(BF16)

    o_ref[...] = _dot(h_ref[...], w_ref[...])


def _proj(x, norm_w, w, tm):
    m = x.shape[0]
    return pl.pallas_call(
        _proj_kernel,
        grid=(m // tm, PROJ_COLS // PROJ_TN),
        in_specs=[
            pl.BlockSpec((tm, D_MODEL), lambda i, j: (i, 0)),
            pl.BlockSpec((1, D_MODEL), lambda i, j: (0, 0)),
            pl.BlockSpec((D_MODEL, PROJ_TN), lambda i, j: (0, j)),
        ],
        out_specs=pl.BlockSpec((tm, PROJ_TN), lambda i, j: (i, j)),
        out_shape=jax.ShapeDtypeStruct((m, PROJ_COLS), F32),
        scratch_shapes=[pltpu.VMEM((tm, D_MODEL), BF16)],
        compiler_params=_cparams(("parallel", "arbitrary")),
        name="proj",
    )(x, norm_w, w)


def _outproj_kernel(x_ref, ya_ref, yb_ref, wa_ref, wb_ref, nw_ref, x1_ref, h2_ref):
    x1 = x_ref[...] + _dot(ya_ref[...], wa_ref[...]) + _dot(yb_ref[...], wb_ref[...])
    x1_ref[...] = x1
    ms = jnp.mean(x1 * x1, axis=-1, keepdims=True)
    h2_ref[...] = (x1 * lax.rsqrt(ms + EPS) * nw_ref[...]).astype(BF16)


def _outproj(x, ya, yb, wa, wb, norm_w, tm):
    m = x.shape[0]
    row = lambda i: (i, 0)
    fixed = lambda i: (0, 0)
    return pl.pallas_call(
        _outproj_kernel,
        grid=(m // tm,),
        in_specs=[
            pl.BlockSpec((tm, D_MODEL), row),
            pl.BlockSpec((tm, A_WIDTH), row),
            pl.BlockSpec((tm, B_WIDTH), row),
            pl.BlockSpec((A_WIDTH, D_MODEL), fixed),
            pl.BlockSpec((B_WIDTH, D_MODEL), fixed),
            pl.BlockSpec((1, D_MODEL), fixed),
        ],
        out_specs=[pl.BlockSpec((tm, D_MODEL), row), pl.BlockSpec((tm, D_MODEL), row)],
        out_shape=[jax.ShapeDtypeStruct((m, D_MODEL), F32),
                   jax.ShapeDtypeStruct((m, D_MODEL), BF16)],
        compiler_params=_cparams(("parallel",)),
        name="outproj",
    )(x, ya, yb, wa, wb, norm_w)


def _ffn1_kernel(h_ref, wg_ref, wu_ref, o_ref):
    h = h_ref[...]
    g = _dot(h, wg_ref[...])
    u = _dot(h, wu_ref[...])
    o_ref[...] = (g * jax.nn.sigmoid(g) * u).astype(BF16)


def _ffn1(h2, wg, wu, tm, tn):
    m = h2.shape[0]
    return pl.pallas_call(
        _ffn1_kernel,
        grid=(m // tm, D_FF // tn),
        in_specs=[
            pl.BlockSpec((tm, D_MODEL), lambda i, j: (i, 0)),
            pl.BlockSpec((D_MODEL, tn), lambda i, j: (0, j)),
            pl.BlockSpec((D_MODEL, tn), lambda i, j: (0, j)),
        ],
        out_specs=pl.BlockSpec((tm, tn), lambda i, j: (i, j)),
        out_shape=jax.ShapeDtypeStruct((m, D_FF), BF16),
        compiler_params=_cparams(("parallel", "parallel")),
        name="ffn1",
    )(h2, wg, wu)


def _ffn2_kernel(g_ref, wd_ref, x1_ref, nw_ref, o_ref):
    k = pl.program_id(1)

    @pl.when(k == 0)
    def _():
        o_ref[...] = x1_ref[...] + _dot(g_ref[...], wd_ref[...])

    @pl.when(jnp.logical_and(k > 0, k < pl.num_programs(1) - 1))
    def _():
        o_ref[...] += _dot(g_ref[...], wd_ref[...])

    @pl.when(k == pl.num_programs(1) - 1)
    def _():
        x2 = o_ref[...] + _dot(g_ref[...], wd_ref[...])
        ms = jnp.mean(x2 * x2, axis=-1, keepdims=True)
        o_ref[...] = x2 * lax.rsqrt(ms + EPS) * nw_ref[...]


def _ffn2(g, wd, x1, norm_w, tm, tk):
    m = g.shape[0]
    assert D_FF // tk >= 2
    return pl.pallas_call(
        _ffn2_kernel,
        grid=(m // tm, D_FF // tk),
        in_specs=[
            pl.BlockSpec((tm, tk), lambda i, k: (i, k)),
            pl.BlockSpec((tk, D_MODEL), lambda i, k: (k, 0)),
            pl.BlockSpec((tm, D_MODEL), lambda i, k: (i, 0)),
            pl.BlockSpec((1, D_MODEL), lambda i, k: (0, 0)),
        ],
        out_specs=pl.BlockSpec((tm, D_MODEL), lambda i, k: (i, 0)),
        out_shape=jax.ShapeDtypeStruct((m, D_MODEL), F32),
        compiler_params=_cparams(("parallel", "arbitrary")),
        name="ffn2",
    )(g, wd, x1, norm_w)


def _shift_rows(z, prev_row):
    rolled = pltpu.roll(z, 1, axis=0)
    row = lax.broadcasted_iota(jnp.int32, z.shape, 0)
    return jnp.where(row == 0, prev_row, rolled)


def _rwkv_program(zr_ref, zk_ref, zv_ref, zl_ref, sh_rkv_ref, sh_l_ref, s0_ref,
                 mu_rkv_ref, mu_l_ref, w0_ref, a0_ref, wa_up_ref, g_up_ref,
                 kk_ref, ka_ref, rk_ref, lnw_ref, lnb_ref, ones_ref,
                 ya_ref, osh_rkv_ref, osh_l_ref, os_ref,
                 prev_ref, s_ref, ops_ref, aux_ref, plast_ref, *, chunk, pipelined):
    C = chunk
    W = A_WIDTH
    HD = A_HEAD_DIM
    c_idx = pl.program_id(1)
    wslot = c_idx % 2 if pipelined else 0
    rslot = 1 - wslot if pipelined else 0
    PR = range(A_PAIRS)
    cols = lambda t, p: t[:, p * LANES:(p + 1) * LANES]
    OP_AT, OP_RT, OP_KH, OP_BH, OP_KE, OP_BE, OP_V = range(7)
    AUX_BONUS, AUX_GATE = range(2)

    @pl.when(c_idx == 0)
    def _():
        prev_ref[:, 0:3 * W] = sh_rkv_ref[0]
        prev_ref[:, 3 * W:3 * W + LR_PAD] = sh_l_ref[0]
        zero = jnp.zeros((HD, HD), F32)
        for p in PR:
            s_ref[p] = jnp.concatenate(
                [jnp.concatenate([s0_ref[0, 2 * p], zero], axis=1),
                 jnp.concatenate([zero, s0_ref[0, 2 * p + 1]], axis=1)], axis=0)
        if pipelined:
            ops_ref[1] = jnp.zeros(ops_ref.shape[1:], BF16)
            aux_ref[1] = jnp.zeros(aux_ref.shape[1:], F32)
            plast_ref[1] = jnp.ones(plast_ref.shape[1:], F32)

    yield
    ti2 = lax.broadcasted_iota(jnp.int32, (C, 2 * C), 0)
    lane_tt = lax.broadcasted_iota(jnp.int32, (C, 2 * C), 1)
    sj2 = jnp.where(lane_tt >= C, lane_tt - C, lane_tt)
    incl2 = sj2 <= ti2
    strict2 = sj2 < ti2
    tri2_b = jnp.where(incl2, 1.0, 0.0).astype(BF16)
    lane_c = lax.broadcasted_iota(jnp.int32, (C, LANES), 1)
    bd_i = lax.broadcasted_iota(jnp.int32, (LANES, LANES), 0) // HD
    bd_j = lax.broadcasted_iota(jnp.int32, (LANES, LANES), 1) // HD
    bd_mask = bd_i == bd_j
    n_sq = int(math.log2(C)) - 1
    bd_ch = lambda x: _block_diag_rhs(x, lane_c, HD)
    bd_tt = lambda x: _block_diag_rhs(x, lane_tt, C)

    def mixed(z, prev_lo, mu):
        width = z.shape[1]
        zp = _shift_rows(z, prev_ref[:, prev_lo:prev_lo + width])
        prev_ref[:, prev_lo:prev_lo + width] = z[C - 1:C, :]
        return z + mu * (zp - z)

    zl_raw = zl_ref[...]
    zl = mixed(zl_raw, 3 * W, mu_l_ref[...])
    osh_l_ref[0] = zl_raw[C - 1:C, :]
    l1 = zl[:, 0:LANES]
    l1 = jnp.where(lane_c < DECAY_RANK, jnp.tanh(l1), l1)
    wa = _dot(l1.astype(BF16), wa_up_ref[...])
    gl = _sigmoid(zl[:, LANES:3 * LANES])
    aux_ref[wslot, AUX_GATE] = _dot(gl.astype(BF16), g_up_ref[...])

    def hsum_tiles(tiles):
        s = _dot(jnp.concatenate(tiles, axis=0).astype(BF16), ones_ref[...])
        return [s[i * C:(i + 1) * C] for i in range(len(tiles))]

    def prepare_tile(p):
        lo, hi = p * LANES, (p + 1) * LANES
        sl = slice(lo, hi)

        def mixed_rkv(z_ref, base):
            z = z_ref[:, sl]
            osh_rkv_ref[0, :, base + lo:base + hi] = z[C - 1:C, :]
            return mixed(z, base + lo, mu_rkv_ref[:, base + lo:base + hi])

        hsum = lambda x: hsum_tiles([x])[0]
        r = mixed_rkv(zr_ref, 0)
        k = mixed_rkv(zk_ref, W)
        v = mixed_rkv(zv_ref, 2 * W)
        wlog = -jax.nn.softplus(-(w0_ref[:, sl] + wa[:, sl])) - 0.5
        lw = -jnp.exp(wlog)
        a = _sigmoid(a0_ref[:, sl] + wa[:, W + lo:W + hi])
        kk = k * kk_ref[:, sl]
        k2 = k * (1.0 + (a - 1.0) * ka_ref[:, sl])
        kk = kk * lax.rsqrt(jnp.maximum(hsum(kk * kk), 1e-24))
        ka = kk * a
        lw_hi = lw.astype(BF16)
        lw_lo = (lw - lw_hi.astype(F32)).astype(BF16)
        cum = _dot(tri2_b, jnp.concatenate([lw_hi, lw_lo], axis=0))
        cum_last = cum[C - 1:C, :]
        e_neg = jnp.exp(-cum)
        e_end = jnp.exp(cum_last - cum)
        ops_ref[wslot, OP_AT, :, sl] = (kk * jnp.exp(cum - lw)).astype(BF16)
        ops_ref[wslot, OP_RT, :, sl] = (r * jnp.exp(cum)).astype(BF16)
        ops_ref[wslot, OP_KH, :, sl] = (k2 * e_neg).astype(BF16)
        ops_ref[wslot, OP_BH, :, sl] = (ka * e_neg).astype(BF16)
        ops_ref[wslot, OP_KE, :, sl] = (k2 * e_end).astype(BF16)
        ops_ref[wslot, OP_BE, :, sl] = (-ka * e_end).astype(BF16)
        ops_ref[wslot, OP_V, :, sl] = v.astype(BF16)
        plast_ref[wslot, :, sl] = jnp.exp(cum_last)
        aux_ref[wslot, AUX_BONUS, :, sl] = hsum(r * k2 * rk_ref[:, sl]) * v

    pending = list(PR)

    def prepare_next():
        if pending:
            prepare_tile(pending.pop(0))

    if not pipelined:
        while pending:
            prepare_next()

    op = lambda i, p: ops_ref[rslot, i, :, p * LANES:(p + 1) * LANES]
    s_old = [s_ref[p] for p in PR]
    v_b = [op(OP_V, p) for p in PR]
    lhs = [jnp.concatenate([op(OP_AT, p), op(OP_RT, p)], axis=0) for p in PR]
    rhs = [jnp.concatenate([bd_ch(op(OP_KH, p)), bd_ch(op(OP_BH, p))], axis=0) for p in PR]
    upd_rhs = [jnp.concatenate([op(OP_KE, p), op(OP_BE, p)], axis=0) for p in PR]
    p_last = plast_ref[rslot]
    m1 = [lax.dot_general(lhs[p], rhs[p], NT, preferred_element_type=F32) for p in PR]
    m2 = [lax.dot_general(lhs[p], s_old[p].astype(BF16), NT, preferred_element_type=F32) for p in PR]
    prepare_next()
    yield
    a_k = [jnp.where(strict2, m1[p][0:C, 0:2 * C], 0.0).astype(BF16) for p in PR]
    a_b = [jnp.where(strict2, m1[p][0:C, 2 * C:4 * C], 0.0) for p in PR]
    b_kb = [jnp.concatenate([jnp.where(incl2, m1[p][C:2 * C, 0:2 * C], 0.0),
                             jnp.where(incl2, -m1[p][C:2 * C, 2 * C:4 * C], 0.0)], axis=1).astype(BF16)
            for p in PR]
    bd_v = [bd_ch(v_b[p]) for p in PR]
    u = [m2[p][0:C] + _dot(a_k[p], bd_v[p]) for p in PR]
    prepare_next()
    yield
    pw_b = [a_b[p].astype(BF16) for p in PR]
    u = [u[p] - _dot(pw_b[p], bd_ch(u[p].astype(BF16))) for p in PR]
    prepare_next()
    yield
    for _ in range(n_sq):
        pw_b = [_dot(pw_b[p], bd_tt(pw_b[p])).astype(BF16) for p in PR]
        u = [u[p] + _dot(pw_b[p], bd_ch(u[p].astype(BF16))) for p in PR]
        prepare_next()
        yield
    u_b = [u[p].astype(BF16) for p in PR]
    ys = [m2[p][C:2 * C] + _dot(b_kb[p], jnp.concatenate([bd_v[p], bd_ch(u_b[p])], axis=0)) for p in PR]
    upd = [lax.dot_general(jnp.concatenate([v_b[p], u_b[p]], axis=0), upd_rhs[p],
                           TN, preferred_element_type=F32) for p in PR]
    for p in PR:
        s_ref[p] = s_old[p] * cols(p_last, p) + jnp.where(bd_mask, upd[p], 0.0)
    while pending:
        prepare_next()
    yield

    inv_n = 1.0 / A_HEAD_DIM
    mean = hsum_tiles(ys)
    d = [ys[p] - mean[p] * inv_n for p in PR]
    var = hsum_tiles([d[p] * d[p] for p in PR])
    yn = jnp.concatenate([d[p] * lax.rsqrt(var[p] * inv_n + LNX_EPS) for p in PR], axis=1)
    y = yn * lnw_ref[...] + lnb_ref[...] + aux_ref[rslot, AUX_BONUS]
    ya_ref[...] = (y * aux_ref[rslot, AUX_GATE]).astype(BF16)
    yield MAIN_DONE

    @pl.when(c_idx == pl.num_programs(1) - 1)
    def _():
        for p in PR:
            s_pair = s_ref[p]
            os_ref[0, 2 * p] = s_pair[0:HD, 0:HD]
            os_ref[0, 2 * p + 1] = s_pair[HD:2 * HD, HD:2 * HD]


def _mamba_program(z_ref, x_ref, bc_ref, dt_ref, cx0_ref, cbc0_ref, s0_ref,
                  cwx_ref, cwbc_ref, cbx_ref, cbbc_ref, dtb_ref, alog_ref, dsk_ref, nw_ref, e3_ref,
                  yb_ref, ocx_ref, ocbc_ref, os_ref,
                  cx_ref, cbc_ref, s_ref, f32_ref, b16_ref, bcs_ref, sdec_ref, *, chunk, pipelined):
    C = chunk
    c_idx = pl.program_id(1)
    wslot = c_idx % 2 if pipelined else 0
    rslot = 1 - wslot if pipelined else 0
    GW = B_WIDTH // B_GROUPS
    HD = B_HEAD_DIM
    PR = range(B_PAIRS)
    cols = lambda t, p: t[:, p * LANES:(p + 1) * LANES]
    F_SKIP, F_GATE, F_ACS = range(3)
    B_XDT, B_XEND = range(2)

    @pl.when(c_idx == 0)
    def _():
        cx_ref[0:CONV_ROWS, :] = cx0_ref[0]
        cbc_ref[0:CONV_ROWS, :] = cbc0_ref[0]
        for p in PR:
            s_ref[:, p * LANES:(p + 1) * LANES] = jnp.concatenate(
                [s0_ref[0, 2 * p].T, s0_ref[0, 2 * p + 1].T], axis=1)
        if pipelined:
            f32_ref[1] = jnp.zeros(f32_ref.shape[1:], F32)
            b16_ref[1] = jnp.zeros(b16_ref.shape[1:], BF16)
            bcs_ref[1] = jnp.zeros(bcs_ref.shape[1:], BF16)
            sdec_ref[1] = jnp.ones(sdec_ref.shape[1:], F32)

    yield
    silu = lambda t: t * _sigmoid(t)

    def causal_conv(raw, ext_ref, taps_ref, bias_ref, sl):
        ext_ref[CONV_ROWS:CONV_ROWS + C, sl] = raw
        acc = bias_ref[:, sl] + taps_ref[CONV_W - 1:CONV_W, sl] * raw
        for j in range(CONV_W - 1):
            start = CONV_ROWS - (CONV_W - 1 - j)
            acc = acc + taps_ref[j:j + 1, sl] * ext_ref[start:start + C, sl]
        ext_ref[0:CONV_ROWS, sl] = raw[C - CONV_ROWS:C, :]
        return silu(acc)

    lane_c = lax.broadcasted_iota(jnp.int32, (C, LANES), 1)
    valid = lane_c < B_HEADS
    dt = jnp.where(valid, jax.nn.softplus(dt_ref[...] + dtb_ref[...]), 0.0)
    d_a = dt * (-jnp.exp(alog_ref[...]))

    ti3 = lax.broadcasted_iota(jnp.int32, (C, 3 * C), 0)
    sj3 = lax.broadcasted_iota(jnp.int32, (C, 3 * C), 1)
    sj3 = jnp.where(sj3 >= 2 * C, sj3 - 2 * C, jnp.where(sj3 >= C, sj3 - C, sj3))
    tri3 = jnp.where(sj3 <= ti3, 1.0, 0.0).astype(BF16)
    a_cs = _dot(tri3, jnp.concatenate(_split3(d_a), axis=0))

    def pack3(t):
        hi, mid, lo = _split3(t)
        return (hi.astype(F32) + pltpu.roll(mid.astype(F32), B_HEADS, axis=1)
                + pltpu.roll(lo.astype(F32), 2 * B_HEADS, axis=1)).astype(BF16)

    expanded = _dot(jnp.concatenate([pack3(dt), pack3(a_cs)], axis=0), e3_ref[...])
    dt_e = expanded[0:C]
    acs_new = expanded[C:2 * C]
    alast_new = acs_new[C - 1:C, :]
    f32_ref[wslot, F_ACS] = acs_new
    sdec_ref[wslot] = jnp.exp(alast_new)

    def prepare_bc():
        bcs_ref[wslot] = causal_conv(bc_ref[...], cbc_ref, cwbc_ref, cbbc_ref,
                                     slice(0, BC_WIDTH)).astype(BF16)
        ocbc_ref[0] = cbc_ref[0:CONV_ROWS, :]

    def prepare_tile(t):
        sl = slice(t * LANES, (t + 1) * LANES)
        xs = causal_conv(x_ref[:, sl], cx_ref, cwx_ref, cbx_ref, sl)
        ocx_ref[0, :, sl] = cx_ref[0:CONV_ROWS, sl]
        xdt = xs * dt_e[:, sl]
        b16_ref[wslot, B_XDT, :, sl] = xdt.astype(BF16)
        b16_ref[wslot, B_XEND, :, sl] = (xdt * jnp.exp(alast_new[:, sl] - acs_new[:, sl])).astype(BF16)
        f32_ref[wslot, F_SKIP, :, sl] = xs * dsk_ref[:, sl]
        f32_ref[wslot, F_GATE, :, sl] = silu(z_ref[:, sl])

    pending = [prepare_bc] + [functools.partial(prepare_tile, t) for t in PR]

    def prepare_next():
        if pending:
            pending.pop(0)()

    if not pipelined:
        while pending:
            prepare_next()
            yield
    yield

    ti2 = lax.broadcasted_iota(jnp.int32, (C, 2 * C), 0)
    lane_tt = lax.broadcasted_iota(jnp.int32, (C, 2 * C), 1)
    sj2 = jnp.where(lane_tt >= C, lane_tt - C, lane_tt)
    causal2 = sj2 <= ti2
    eye2 = sj2 == ti2

    acs_e = f32_ref[rslot, F_ACS]
    s_decay = sdec_ref[rslot]
    bcm = bcs_ref[rslot]
    cb2, y_state = [], []
    for g in range(B_GROUPS):
        bm = bcm[:, g * D_STATE:(g + 1) * D_STATE]
        cm = bcm[:, (B_GROUPS + g) * D_STATE:(B_GROUPS + g + 1) * D_STATE]
        cb = lax.dot_general(cm, bm, NT, preferred_element_type=F32)
        cb2.append(jnp.concatenate([cb, cb], axis=1))
        gsl = slice(g * GW, (g + 1) * GW)
        s_old = s_ref[:, gsl]
        y_state.append(_dot(cm, s_old.astype(BF16)) * jnp.exp(acs_e[:, gsl]))
        s_ref[:, gsl] = s_old * s_decay[:, gsl] + lax.dot_general(
            bm, b16_ref[rslot, B_XEND, :, gsl], TN, preferred_element_type=F32)
        prepare_next()
        yield
    y_state = jnp.concatenate(y_state, axis=1)

    ys = []
    for p in PR:
        if C == HD:
            col_form = cols(acs_e, p)
        else:
            col_form = jnp.concatenate([acs_e[:, p * LANES:p * LANES + C],
                                        acs_e[:, p * LANES + HD:p * LANES + HD + C]], axis=1)
        row_form = jnp.sum(jnp.where(eye2, col_form, 0.0), axis=0, keepdims=True)
        m = jnp.where(causal2, jnp.exp(col_form - row_form), 0.0) * cb2[p // (B_PAIRS // B_GROUPS)]
        xdt_p = b16_ref[rslot, B_XDT, :, p * LANES:(p + 1) * LANES]
        ys.append(_dot(m.astype(BF16), _block_diag_rhs(xdt_p, lane_c, HD)))
        prepare_next()
        yield
    while pending:
        prepare_next()
    y = jnp.concatenate(ys, axis=1) + y_state

    y = (y + f32_ref[rslot, F_SKIP]) * f32_ref[rslot, F_GATE]
    outs = []
    for g in range(B_GROUPS):
        yg = y[:, g * GW:(g + 1) * GW]
        ms = jnp.mean(yg * yg, axis=-1, keepdims=True)
        outs.append(yg * lax.rsqrt(ms + EPS))
    yb_ref[...] = (jnp.concatenate(outs, axis=1) * nw_ref[...]).astype(BF16)
    yield MAIN_DONE

    @pl.when(c_idx == pl.num_programs(1) - 1)
    def _():
        for h in range(B_HEADS):
            os_ref[0, h] = s_ref[:, h * HD:(h + 1) * HD].T


N_PROJ_BLOCKS = 4
N_STATE = 3
N_MIXER_OUT = 1 + N_STATE


def _mixer_kernel(*refs, chunk, pipelined, streams, stream_view, shared_state, n_rwkv_wts, n_mamba_wts,
                  n_rwkv_scratch):
    refs = list(refs)
    n_mixer_in = N_PROJ_BLOCKS + N_STATE
    n_rwkv_in = n_mixer_in + n_rwkv_wts
    n_in = n_rwkv_in + n_mixer_in + n_mamba_wts
    outs = refs[n_in:n_in + 2 * N_MIXER_OUT]
    scratch = refs[n_in + 2 * N_MIXER_OUT:]
    one = lambda ref, s: ref.at[pl.ds(s, 1)]

    def mixer_refs(ins, outs, scratch, s):
        proj = [stream_view(r, s) for r in ins[:N_PROJ_BLOCKS]]
        state = [r if shared_state else one(r, s) for r in ins[N_PROJ_BLOCKS:n_mixer_in]]
        return (proj + state + ins[n_mixer_in:] + [stream_view(outs[0], s)]
                + [one(r, s) for r in outs[1:]] + [r.at[s] for r in scratch])

    programs = []
    for s in range(streams):
        programs.append(_rwkv_program(
            *mixer_refs(refs[:n_rwkv_in], outs[:N_MIXER_OUT], scratch[:n_rwkv_scratch], s),
            chunk=chunk, pipelined=pipelined))
        programs.append(_mamba_program(
            *mixer_refs(refs[n_rwkv_in:n_in], outs[N_MIXER_OUT:], scratch[n_rwkv_scratch:], s),
            chunk=chunk, pipelined=pipelined))
    main_done = [False] * len(programs)
    while not all(main_done):
        for i, prog in enumerate(programs):
            if not main_done[i]:
                main_done[i] = next(prog) is MAIN_DONE
    for prog in programs:
        for _ in prog:
            pass


def _mixers(proj, row_off, nb, nc, chunk, rwkv_state, mamba_state, shared_state, rwkv_wts, mamba_wts):
    C = chunk
    assert CONV_ROWS <= C <= B_HEAD_DIM
    W = A_WIDTH
    S = 2 if nb % 2 == 0 else 1
    pipelined, in_chunk, out_chunk = _chunk_pipeline(nc)
    slots = 2 if pipelined else 1
    if nc == 1:
        assert row_off % (S * C) == 0
        rb = row_off // (S * C)
        zspec = lambda width, col: pl.BlockSpec((S * C, width), lambda b, c: (rb + b, col // width))
        yspec = pl.BlockSpec((S * C, W), lambda b, c: (b, 0))
        y_shape = jax.ShapeDtypeStruct((nb * C, W), BF16)
        stream_view = lambda ref, s: ref.at[pl.ds(s * C, C)]
    else:
        assert row_off == 0 and proj.shape[0] == nb * nc * C
        proj = proj.reshape(nb, nc * C, PROJ_COLS)
        zspec = lambda width, col: pl.BlockSpec(
            (S, C, width), lambda b, c: (b, in_chunk(c), col // width))
        yspec = pl.BlockSpec((S, C, W), lambda b, c: (b, out_chunk(c), 0))
        y_shape = jax.ShapeDtypeStruct((nb, nc * C, W), BF16)
        stream_view = lambda ref, s: ref.at[s]
    full = lambda arr: pl.BlockSpec(arr.shape, lambda b, c: (0,) * arr.ndim)

    def state_specs(blocks):
        zeros = lambda blk: (0,) * (len(blk) - 1)
        if shared_state:
            ins = [pl.BlockSpec(blk, lambda b, c, z=zeros(blk): (0,) + z) for blk in blocks]
        else:
            ins = [pl.BlockSpec((S,) + blk[1:], lambda b, c, z=zeros(blk): (b,) + z) for blk in blocks]
        outs = [pl.BlockSpec((S,) + blk[1:], lambda b, c, z=zeros(blk): (b,) + z) for blk in blocks]
        shapes = [jax.ShapeDtypeStruct((nb,) + blk[1:], F32) for blk in blocks]
        return ins, outs, shapes

    r_in, r_out, r_shapes = state_specs(
        [(1, 1, 3 * W), (1, 1, LR_PAD), (1, A_HEADS, A_HEAD_DIM, A_HEAD_DIM)])
    m_in, m_out, m_shapes = state_specs(
        [(1, CONV_ROWS, B_WIDTH), (1, CONV_ROWS, BC_WIDTH), (1, B_HEADS, B_HEAD_DIM, D_STATE)])
    rwkv_in = [zspec(W, COL_R), zspec(W, COL_K), zspec(W, COL_V), zspec(LR_PAD, COL_LR)] + r_in \
        + [full(w) for w in rwkv_wts]
    mamba_in = [zspec(B_WIDTH, COL_Z), zspec(B_WIDTH, COL_X), zspec(BC_WIDTH, COL_BC),
                zspec(DT_PAD, COL_DT)] + m_in + [full(w) for w in mamba_wts]
    per_stream = lambda shape, dtype: pltpu.VMEM((S,) + shape, dtype)
    rwkv_scratch = [
        per_stream((1, 3 * W + LR_PAD), F32),
        per_stream((A_PAIRS, LANES, LANES), F32),
        per_stream((slots, 7, C, W), BF16),
        per_stream((slots, 2, C, W), F32),
        per_stream((slots, 1, W), F32),
    ]
    mamba_scratch = [
        per_stream((CONV_ROWS + C, B_WIDTH), F32),
        per_stream((CONV_ROWS + C, BC_WIDTH), F32),
        per_stream((D_STATE, B_WIDTH), F32),
        per_stream((slots, 3, C, B_WIDTH), F32),
        per_stream((slots, 2, C, B_WIDTH), BF16),
        per_stream((slots, C, BC_WIDTH), BF16),
        per_stream((slots, 1, B_WIDTH), F32),
    ]
    outs = pl.pallas_call(
        functools.partial(_mixer_kernel, chunk=C, pipelined=pipelined, streams=S,
                          stream_view=stream_view, shared_state=shared_state,
                          n_rwkv_wts=len(rwkv_wts), n_mamba_wts=len(mamba_wts),
                          n_rwkv_scratch=len(rwkv_scratch)),
        grid=(nb // S, nc + 1 if pipelined else nc),
        in_specs=rwkv_in + mamba_in,
        out_specs=[yspec] + r_out + [yspec] + m_out,
        out_shape=[y_shape] + r_shapes + [y_shape] + m_shapes,
        scratch_shapes=rwkv_scratch + mamba_scratch,
        compiler_params=_cparams(("parallel", "arbitrary")),
        name="mixers",
    )(*[proj] * N_PROJ_BLOCKS, *rwkv_state, *rwkv_wts, *[proj] * N_PROJ_BLOCKS, *mamba_state, *mamba_wts)
    outs = [o.reshape(nb * nc * C, W) if i % N_MIXER_OUT == 0 else o for i, o in enumerate(outs)]
    return outs[:N_MIXER_OUT], outs[N_MIXER_OUT:]


def _pad_cols(a, width):
    return jnp.pad(a, [(0, 0)] * (a.ndim - 1) + [(0, width - a.shape[-1])])


def kernel(x_prompt, x_sample, state_rwkv_shift, state_rwkv_wkv, state_ssm_conv, state_ssm, meta_tokens, norm_mix_w, w_in, rwkv_mu, rwkv_w0, rwkv_w_up, rwkv_a0, rwkv_a_up, rwkv_g_up, rwkv_k_k, rwkv_k_a, rwkv_r_k, rwkv_lnx_w, rwkv_lnx_b, ssm_conv_w, ssm_conv_b, ssm_dt_bias, ssm_A_log, ssm_D, ssm_norm_w, w_out, norm_ffn_w, ffn_w_gate, ffn_w_up, ffn_w_down, norm_final_w):
    depth = w_in.shape[0]
    assert depth == 1, "single-layer step"
    bp, lp, d = x_prompt.shape
    bs, ls, _ = x_sample.shape
    n_meta = meta_tokens.shape[0]
    assert d == D_MODEL and lp % CHUNK == 0 and ls <= CHUNK and n_meta <= CHUNK
    assert w_in.shape[-1] == A_COLS + 2 * B_WIDTH + BC_WIDTH + B_HEADS
    W = A_WIDTH

    wi = w_in[0]
    w_perm = jnp.concatenate([
        wi[:, 0:3 * W],
        wi[:, A_COLS:A_COLS + 2 * B_WIDTH + BC_WIDTH],
        _pad_cols(wi[:, 3 * W:A_COLS], LR_PAD),
        _pad_cols(wi[:, A_COLS + 2 * B_WIDTH + BC_WIDTH:], DT_COLS),
    ], axis=1).astype(BF16)
    nmw = norm_mix_w.reshape(1, D_MODEL)
    mu = rwkv_mu[0]
    mu_rkv = mu[None, 0:3 * W]
    mu_l = _pad_cols(mu[None, 3 * W:], LR_PAD)
    wa_up = jnp.zeros((LANES, 2 * W), F32)
    wa_up = wa_up.at[0:DECAY_RANK, 0:W].set(rwkv_w_up[0])
    wa_up = wa_up.at[DECAY_RANK:DECAY_RANK + AAA_RANK, W:].set(rwkv_a_up[0]).astype(BF16)
    g_up = jnp.pad(rwkv_g_up[0], ((0, 2 * LANES - GATE_RANK), (0, 0))).astype(BF16)
    lane_head = jnp.arange(LANES, dtype=jnp.int32) // A_HEAD_DIM
    ones_bd = (lane_head[:, None] == lane_head[None, :]).astype(BF16)
    rwkv_wts = (mu_rkv, mu_l, rwkv_w0.reshape(1, W), rwkv_a0.reshape(1, W), wa_up, g_up,
                rwkv_k_k.reshape(1, W), rwkv_k_a.reshape(1, W), rwkv_r_k.reshape(1, W),
                rwkv_lnx_w.reshape(1, W), rwkv_lnx_b.reshape(1, W), ones_bd)
    taps = ssm_conv_w[0].T
    cbias = ssm_conv_b.reshape(1, -1)
    src_head = jnp.where(jnp.arange(LANES) < 3 * B_HEADS, jnp.arange(LANES) % B_HEADS, -1)
    e3 = (src_head[:, None] == (jnp.arange(B_WIDTH) // B_HEAD_DIM)[None, :]).astype(BF16)
    mamba_wts = (taps[:, :B_WIDTH], taps[:, B_WIDTH:], cbias[:, :B_WIDTH], cbias[:, B_WIDTH:],
                 _pad_cols(ssm_dt_bias.reshape(1, B_HEADS), DT_PAD),
                 _pad_cols(ssm_A_log.reshape(1, B_HEADS), DT_PAD),
                 jnp.repeat(ssm_D.reshape(1, B_HEADS), B_HEAD_DIM, axis=1),
                 ssm_norm_w.reshape(1, B_WIDTH), e3)
    wo = w_out[0].astype(BF16)
    wo_a, wo_b = wo[:W], wo[W:]
    nfw = norm_ffn_w.reshape(1, D_MODEL)
    wg = ffn_w_gate[0].astype(BF16)
    wu = ffn_w_up[0].astype(BF16)
    wd = ffn_w_down[0].astype(BF16)
    nlw = norm_final_w.reshape(1, D_MODEL)

    tm_small = 384
    n_small = bs * ls + n_meta
    m_small = -(-n_small // tm_small) * tm_small
    x_small = jnp.concatenate([x_sample.reshape(bs * ls, d), meta_tokens,
                               jnp.zeros((m_small - n_small, d), F32)], axis=0)
    x_big = x_prompt.reshape(bp * lp, d)
    tm_big = math.gcd(bp * lp, 512)

    proj_small = _proj(x_small, nmw, w_perm, m_small)
    proj_big = _proj(x_big, nmw, w_perm, math.gcd(bp * lp, 1024))

    zeros = lambda *s: jnp.zeros(s, F32)
    sh = state_rwkv_shift[0]
    smp_rwkv = (sh[:, None, 0:3 * W], _pad_cols(sh[:, None, 3 * W:], LR_PAD), state_rwkv_wkv[0])
    conv0 = jnp.pad(state_ssm_conv[0], ((0, 0), (CONV_ROWS - (CONV_W - 1), 0), (0, 0)))
    smp_mamba = (conv0[..., :B_WIDTH], conv0[..., B_WIDTH:], state_ssm[0])
    meta_rwkv0 = (zeros(1, 1, 3 * W), zeros(1, 1, LR_PAD), zeros(1, A_HEADS, A_HEAD_DIM, A_HEAD_DIM))
    meta_mamba0 = (zeros(1, CONV_ROWS, B_WIDTH), zeros(1, CONV_ROWS, BC_WIDTH),
                   zeros(1, B_HEADS, B_HEAD_DIM, D_STATE))

    (ya_m, *meta_rwkv), (yb_m, *meta_mamba) = _mixers(
        proj_small, bs * ls, 1, 1, n_meta, meta_rwkv0, meta_mamba0, False, rwkv_wts, mamba_wts)
    (ya_s, *s_rwkv), (yb_s, *s_mamba) = _mixers(
        proj_small, 0, bs, 1, ls, smp_rwkv, smp_mamba, False, rwkv_wts, mamba_wts)
    (ya_p, *p_rwkv), (yb_p, *p_mamba) = _mixers(
        proj_big, 0, bp, lp // CHUNK, CHUNK, meta_rwkv, meta_mamba, True, rwkv_wts, mamba_wts)

    def tail(ya, yb, x, tm, tm_ffn1):
        x1, h2 = _outproj(x, ya, yb, wo_a, wo_b, nfw, tm)
        g = _ffn1(h2, wg, wu, tm_ffn1, 512)
        return _ffn2(g, wd, x1, nlw, tm, D_FF // 2)

    pad_rows = lambda *parts: jnp.concatenate(
        list(parts) + [jnp.zeros((m_small - n_small, parts[0].shape[1]), parts[0].dtype)], axis=0)
    y_small = tail(pad_rows(ya_s, ya_m), pad_rows(yb_s, yb_m), x_small, tm_small, m_small)
    y_big = tail(ya_p, yb_p, x_big, tm_big, math.gcd(bp * lp, 1024))

    def states_out(rw, mb):
        sh_rkv, sh_l, wkv = rw
        cx, cbc, ssm = mb
        shift = jnp.concatenate([sh_rkv[:, 0], sh_l[:, 0, :LOWRANK]], axis=-1)[None]
        conv = jnp.concatenate([cx, cbc], axis=-1)[None, :, CONV_ROWS - (CONV_W - 1):]
        return shift, wkv[None], conv, ssm[None]

    y_prompt = y_big.reshape(bp, lp, d)
    y_sample = y_small[:bs * ls].reshape(bs, ls, d)
    return (y_prompt, y_sample) + states_out(p_rwkv, p_mamba) + states_out(s_rwkv, s_mamba)
```

```python
import functools
import math

import jax
import jax.numpy as jnp
from jax import lax
from jax.experimental import pallas as pl
from jax.experimental.pallas import tpu as pltpu

F32 = jnp.float32
BF16 = jnp.bfloat16

D_MODEL = 2048
CHUNK = 64
EPS = 1e-6
LNX_EPS = 64e-5
A_WIDTH = 1024
A_HEADS = 16
A_HEAD_DIM = 64
A_PAIRS = A_HEADS // 2
DECAY_RANK = 64
AAA_RANK = 64
GATE_RANK = 160
B_WIDTH = 1024
B_HEADS = 16
B_HEAD_DIM = 64
B_PAIRS = B_HEADS // 2
B_GROUPS = 2
D_STATE = 128
CONV_W = 4
BC_WIDTH = 2 * B_GROUPS * D_STATE
A_COLS = 3 * A_WIDTH + DECAY_RANK + AAA_RANK + GATE_RANK
LOWRANK = DECAY_RANK + AAA_RANK + GATE_RANK
D_FF = 5632

LANES = 128
SUBLANES = 8
VMEM_LIMIT = 56 * 1024 * 1024

LR_PAD = 512
DT_PAD = 128
DT_COLS = 256
COL_R, COL_K, COL_V, COL_Z, COL_X = 0, 1024, 2048, 3072, 4096
COL_BC = 5120
COL_LR = 5632
COL_DT = 6144
PROJ_COLS = COL_DT + DT_COLS
PROJ_TN = 1280
CONV_ROWS = SUBLANES

NT = (((1,), (1,)), ((), ()))
TN = (((0,), (0,)), ((), ()))


def _cparams(sem):
    return pltpu.CompilerParams(dimension_semantics=sem, vmem_limit_bytes=VMEM_LIMIT)


def _dot(x, y):
    return jnp.dot(x, y, preferred_element_type=F32)


def _block_diag_rhs(x, lane, split):
    zero = jnp.zeros_like(x)
    return jnp.concatenate([jnp.where(lane < split, x, zero), jnp.where(lane < split, zero, x)], axis=0)


MAIN_DONE = "main done"


def _sigmoid(t):
    return 0.5 + 0.5 * jnp.tanh(0.5 * t)


def _chunk_pipeline(nc):
    if nc > 1:
        return True, (lambda c: jnp.minimum(c, nc - 1)), (lambda c: jnp.maximum(c - 1, 0))
    return False, (lambda c: c), (lambda c: c)


def _split3(x):
    hi = x.astype(BF16)
    r1 = x - hi.astype(F32)
    mid = r1.astype(BF16)
    return hi, mid, (r1 - mid.astype(F32)).astype(BF16)


def _proj_kernel(x_ref, nw_ref, w_ref, o_ref, h_ref):
    @pl.when(pl.program_id(1) == 0)
    def _():
        x = x_ref[...]
        ms = jnp.mean(x * x, axis=-1, keepdims=True)
        h_ref[...] = (x * lax.rsqrt(ms + EPS) * nw_ref[...]).astype(BF16)

    o_ref[...] = _dot(h_ref[...], w_ref[...])


def _proj(x, norm_w, w, tm):
    m = x.shape[0]
    return pl.pallas_call(
        _proj_kernel,
        grid=(m // tm, PROJ_COLS // PROJ_TN),
        in_specs=[
            pl.BlockSpec((tm, D_MODEL), lambda i, j: (i, 0)),
            pl.BlockSpec((1, D_MODEL), lambda i, j: (0, 0)),
            pl.BlockSpec((D_MODEL, PROJ_TN), lambda i, j: (0, j)),
        ],
        out_specs=pl.BlockSpec((tm, PROJ_TN), lambda i, j: (i, j)),
        out_shape=jax.ShapeDtypeStruct((m, PROJ_COLS), F32),
        scratch_shapes=[pltpu.VMEM((tm, D_MODEL), BF16)],
        compiler_params=_cparams(("parallel", "arbitrary")),
        name="proj",
    )(x, norm_w, w)


def _outproj_kernel(x_ref, ya_ref, yb_ref, wa_ref, wb_ref, nw_ref, x1_ref, h2_ref):
    x1 = x_ref[...] + _dot(ya_ref[...], wa_ref[...]) + _dot(yb_ref[...], wb_ref[...])
    x1_ref[...] = x1
    ms = jnp.mean(x1 * x1, axis=-1, keepdims=True)
    h2_ref[...] = (x1 * lax.rsqrt(ms + EPS) * nw_ref[...]).astype(BF16)


def _outproj(x, ya, yb, wa, wb, norm_w, tm):
    m = x.shape[0]
    row = lambda i: (i, 0)
    fixed = lambda i: (0, 0)
    return pl.pallas_call(
        _outproj_kernel,
        grid=(m // tm,),
        in_specs=[
            pl.BlockSpec((tm, D_MODEL), row),
            pl.BlockSpec((tm, A_WIDTH), row),
            pl.BlockSpec((tm, B_WIDTH), row),
            pl.BlockSpec((A_WIDTH, D_MODEL), fixed),
            pl.BlockSpec((B_WIDTH, D_MODEL), fixed),
            pl.BlockSpec((1, D_MODEL), fixed),
        ],
        out_specs=[pl.BlockSpec((tm, D_MODEL), row), pl.BlockSpec((tm, D_MODEL), row)],
        out_shape=[jax.ShapeDtypeStruct((m, D_MODEL), F32),
                   jax.ShapeDtypeStruct((m, D_MODEL), BF16)],
        compiler_params=_cparams(("parallel",)),
        name="outproj",
    )(x, ya, yb, wa, wb, norm_w)


def _ffn1_kernel(h_ref, wg_ref, wu_ref, o_ref):
    h = h_ref[...]
    g = _dot(h, wg_ref[...])
    u = _dot(h, wu_ref[...])
    o_ref[...] = (g * jax.nn.sigmoid(g) * u).astype(BF16)


def _ffn1(h2, wg, wu, tm, tn):
    m = h2.shape[0]
    return pl.pallas_call(
        _ffn1_kernel,
        grid=(m // tm, D_FF // tn),
        in_specs=[
            pl.BlockSpec((tm, D_MODEL), lambda i, j: (i, 0)),
            pl.BlockSpec((D_MODEL, tn), lambda i, j: (0, j)),
            pl.BlockSpec((D_MODEL, tn), lambda i, j: (0, j)),
        ],
        out_specs=pl.BlockSpec((tm, tn), lambda i, j: (i, j)),
        out_shape=jax.ShapeDtypeStruct((m, D_FF), BF16),
        compiler_params=_cparams(("parallel", "parallel")),
        name="ffn1",
    )(h2, wg, wu)


def _ffn2_kernel(g_ref, wd_ref, x1_ref, nw_ref, o_ref):
    k = pl.program_id(1)

    @pl.when(k == 0)
    def _():
        o_ref[...] = x1_ref[...] + _dot(g_ref[...], wd_ref[...])

    @pl.when(jnp.logical_and(k > 0, k < pl.num_programs(1) - 1))
    def _():
        o_ref[...] += _dot(g_ref[...], wd_ref[...])

    @pl.when(k == pl.num_programs(1) - 1)
    def _():
        x2 = o_ref[...] + _dot(g_ref[...], wd_ref[...])
        ms = jnp.mean(x2 * x2, axis=-1, keepdims=True)
        o_ref[...] = x2 * lax.rsqrt(ms + EPS) * nw_ref[...]


def _ffn2(g, wd, x1, norm_w, tm, tk):
    m = g.shape[0]
    assert D_FF // tk >= 2
    return pl.pallas_call(
        _ffn2_kernel,
        grid=(m // tm, D_FF // tk),
        in_specs=[
            pl.BlockSpec((tm, tk), lambda i, k: (i, k)),
            pl.BlockSpec((tk, D_MODEL), lambda i, k: (k, 0)),
            pl.BlockSpec((tm, D_MODEL), lambda i, k: (i, 0)),
            pl.BlockSpec((1, D_MODEL), lambda i, k: (0, 0)),
        ],
        out_specs=pl.BlockSpec((tm, D_MODEL), lambda i, k: (i, 0)),
        out_shape=jax.ShapeDtypeStruct((m, D_MODEL), F32),
        compiler_params=_cparams(("parallel", "arbitrary")),
        name="ffn2",
    )(g, wd, x1, norm_w)


def _shift_rows(z, prev_row):
    rolled = pltpu.roll(z, 1, axis=0)
    row = lax.broadcasted_iota(jnp.int32, z.shape, 0)
    return jnp.where(row == 0, prev_row, rolled)


def _rwkv_program(zr_ref, zk_ref, zv_ref, zl_ref, sh_rkv_ref, sh_l_ref, s0_ref,
                 mu_rkv_ref, mu_l_ref, w0_ref, a0_ref,
                 kk_ref, ka_ref, rk_ref, lnw_ref, lnb_ref, ones_ref,
                 ya_ref, osh_rkv_ref, osh_l_ref, os_ref,
                 prev_ref, s_ref, ops_ref, aux_ref, plast_ref, *, chunk, pipelined):
    C = chunk
    W = A_WIDTH
    HD = A_HEAD_DIM
    c_idx = pl.program_id(1)
    wslot = c_idx % 2 if pipelined else 0
    rslot = 1 - wslot if pipelined else 0
    PR = range(A_PAIRS)
    cols = lambda t, p: t[:, p * LANES:(p + 1) * LANES]
    OP_AT, OP_RT, OP_KH, OP_BH, OP_KE, OP_BE, OP_V = range(7)
    AUX_BONUS, AUX_GATE = range(2)

    @pl.when(c_idx == 0)
    def _():
        prev_ref[:, 0:3 * W] = sh_rkv_ref[0]
        prev_ref[:, 3 * W:3 * W + LR_PAD] = sh_l_ref[0]
        zero = jnp.zeros((HD, HD), F32)
        for p in PR:
            s_ref[p] = jnp.concatenate(
                [jnp.concatenate([s0_ref[0, 2 * p], zero], axis=1),
                 jnp.concatenate([zero, s0_ref[0, 2 * p + 1]], axis=1)], axis=0)
        if pipelined:
            ops_ref[1] = jnp.zeros(ops_ref.shape[1:], BF16)
            aux_ref[1] = jnp.zeros(aux_ref.shape[1:], F32)
            plast_ref[1] = jnp.ones(plast_ref.shape[1:], F32)

    yield
    ti2 = lax.broadcasted_iota(jnp.int32, (C, 2 * C), 0)
    lane_tt = lax.broadcasted_iota(jnp.int32, (C, 2 * C), 1)
    sj2 = jnp.where(lane_tt >= C, lane_tt - C, lane_tt)
    incl2 = sj2 <= ti2
    strict2 = sj2 < ti2
    tri2_b = jnp.where(incl2, 1.0, 0.0).astype(BF16)
    lane_c = lax.broadcasted_iota(jnp.int32, (C, LANES), 1)
    bd_i = lax.broadcasted_iota(jnp.int32, (LANES, LANES), 0) // HD
    bd_j = lax.broadcasted_iota(jnp.int32, (LANES, LANES), 1) // HD
    bd_mask = bd_i == bd_j
    n_sq = int(math.log2(C)) - 1
    bd_ch = lambda x: _block_diag_rhs(x, lane_c, HD)
    bd_tt = lambda x: _block_diag_rhs(x, lane_tt, C)

    def mixed(z, prev_lo, mu):
        width = z.shape[1]
        zp = _shift_rows(z, prev_ref[:, prev_lo:prev_lo + width])
        prev_ref[:, prev_lo:prev_lo + width] = z[C - 1:C, :]
        return z + mu * (zp - z)

    zl_raw = zl_ref[...]
    zl = mixed(zl_raw, 3 * W, mu_l_ref[...])
    osh_l_ref[0] = zl_raw[C - 1:C, :]
    l1 = zl[:, 0:LANES]
    l1 = jnp.where(lane_c < DECAY_RANK, jnp.tanh(l1), l1)
    gl = _sigmoid(zl[:, LANES:3 * LANES])
    wa, gate = yield l1.astype(BF16), gl.astype(BF16)
    aux_ref[wslot, AUX_GATE] = gate

    def hsum_tiles(tiles):
        s = _dot(jnp.concatenate(tiles, axis=0).astype(BF16), ones_ref[...])
        return [s[i * C:(i + 1) * C] for i in range(len(tiles))]

    def prepare_tile(p):
        lo, hi = p * LANES, (p + 1) * LANES
        sl = slice(lo, hi)

        def mixed_rkv(z_ref, base):
            z = z_ref[:, sl]
            osh_rkv_ref[0, :, base + lo:base + hi] = z[C - 1:C, :]
            return mixed(z, base + lo, mu_rkv_ref[:, base + lo:base + hi])

        hsum = lambda x: hsum_tiles([x])[0]
        r = mixed_rkv(zr_ref, 0)
        k = mixed_rkv(zk_ref, W)
        v = mixed_rkv(zv_ref, 2 * W)
        wlog = -jax.nn.softplus(-(w0_ref[:, sl] + wa[:, sl])) - 0.5
        lw = -jnp.exp(wlog)
        a = _sigmoid(a0_ref[:, sl] + wa[:, W + lo:W + hi])
        kk = k * kk_ref[:, sl]
        k2 = k * (1.0 + (a - 1.0) * ka_ref[:, sl])
        kk = kk * lax.rsqrt(jnp.maximum(hsum(kk * kk), 1e-24))
        ka = kk * a
        lw_hi = lw.astype(BF16)
        lw_lo = (lw - lw_hi.astype(F32)).astype(BF16)
        cum = _dot(tri2_b, jnp.concatenate([lw_hi, lw_lo], axis=0))
        cum_last = cum[C - 1:C, :]
        e_neg = jnp.exp(-cum)
        e_end = jnp.exp(cum_last - cum)
        ops_ref[wslot, OP_AT, :, sl] = (kk * jnp.exp(cum - lw)).astype(BF16)
        ops_ref[wslot, OP_RT, :, sl] = (r * jnp.exp(cum)).astype(BF16)
        ops_ref[wslot, OP_KH, :, sl] = (k2 * e_neg).astype(BF16)
        ops_ref[wslot, OP_BH, :, sl] = (ka * e_neg).astype(BF16)
        ops_ref[wslot, OP_KE, :, sl] = (k2 * e_end).astype(BF16)
        ops_ref[wslot, OP_BE, :, sl] = (-ka * e_end).astype(BF16)
        ops_ref[wslot, OP_V, :, sl] = v.astype(BF16)
        plast_ref[wslot, :, sl] = jnp.exp(cum_last)
        aux_ref[wslot, AUX_BONUS, :, sl] = hsum(r * k2 * rk_ref[:, sl]) * v

    pending = list(PR)

    def prepare_next():
        if pending:
            prepare_tile(pending.pop(0))

    if not pipelined:
        while pending:
            prepare_next()

    op = lambda i, p: ops_ref[rslot, i, :, p * LANES:(p + 1) * LANES]
    s_old = [s_ref[p] for p in PR]
    v_b = [op(OP_V, p) for p in PR]
    lhs = [jnp.concatenate([op(OP_AT, p), op(OP_RT, p)], axis=0) for p in PR]
    rhs = [jnp.concatenate([bd_ch(op(OP_KH, p)), bd_ch(op(OP_BH, p))], axis=0) for p in PR]
    upd_rhs = [jnp.concatenate([op(OP_KE, p), op(OP_BE, p)], axis=0) for p in PR]
    p_last = plast_ref[rslot]
    m1 = [lax.dot_general(lhs[p], rhs[p], NT, preferred_element_type=F32) for p in PR]
    m2 = [lax.dot_general(lhs[p], s_old[p].astype(BF16), NT, preferred_element_type=F32) for p in PR]
    prepare_next()
    yield
    a_k = [jnp.where(strict2, m1[p][0:C, 0:2 * C], 0.0).astype(BF16) for p in PR]
    a_b = [jnp.where(strict2, m1[p][0:C, 2 * C:4 * C], 0.0) for p in PR]
    b_kb = [jnp.concatenate([jnp.where(incl2, m1[p][C:2 * C, 0:2 * C], 0.0),
                             jnp.where(incl2, -m1[p][C:2 * C, 2 * C:4 * C], 0.0)], axis=1).astype(BF16)
            for p in PR]
    bd_v = [bd_ch(v_b[p]) for p in PR]
    u = [m2[p][0:C] + _dot(a_k[p], bd_v[p]) for p in PR]
    prepare_next()
    yield
    pw_b = [a_b[p].astype(BF16) for p in PR]
    u = [u[p] - _dot(pw_b[p], bd_ch(u[p].astype(BF16))) for p in PR]
    prepare_next()
    yield
    for _ in range(n_sq):
        pw_b = [_dot(pw_b[p], bd_tt(pw_b[p])).astype(BF16) for p in PR]
        u = [u[p] + _dot(pw_b[p], bd_ch(u[p].astype(BF16))) for p in PR]
        prepare_next()
        yield
    u_b = [u[p].astype(BF16) for p in PR]
    ys = [m2[p][C:2 * C] + _dot(b_kb[p], jnp.concatenate([bd_v[p], bd_ch(u_b[p])], axis=0)) for p in PR]
    upd = [lax.dot_general(jnp.concatenate([v_b[p], u_b[p]], axis=0), upd_rhs[p],
                           TN, preferred_element_type=F32) for p in PR]
    for p in PR:
        s_ref[p] = s_old[p] * cols(p_last, p) + jnp.where(bd_mask, upd[p], 0.0)
    while pending:
        prepare_next()
    yield

    inv_n = 1.0 / A_HEAD_DIM
    mean = hsum_tiles(ys)
    d = [ys[p] - mean[p] * inv_n for p in PR]
    var = hsum_tiles([d[p] * d[p] for p in PR])
    yn = jnp.concatenate([d[p] * lax.rsqrt(var[p] * inv_n + LNX_EPS) for p in PR], axis=1)
    y = yn * lnw_ref[...] + lnb_ref[...] + aux_ref[rslot, AUX_BONUS]
    ya_ref[...] = (y * aux_ref[rslot, AUX_GATE]).astype(BF16)
    yield MAIN_DONE

    @pl.when(c_idx == pl.num_programs(1) - 1)
    def _():
        for p in PR:
            s_pair = s_ref[p]
            os_ref[0, 2 * p] = s_pair[0:HD, 0:HD]
            os_ref[0, 2 * p + 1] = s_pair[HD:2 * HD, HD:2 * HD]


def _mamba_program(z_ref, x_ref, bc_ref, dt_ref, cx0_ref, cbc0_ref, s0_ref,
                  cwx_ref, cwbc_ref, cbx_ref, cbbc_ref, dtb_ref, alog_ref, dsk_ref, nw_ref,
                  yb_ref, ocx_ref, ocbc_ref, os_ref,
                  cx_ref, cbc_ref, s_ref, f32_ref, b16_ref, bcs_ref, sdec_ref, *, chunk, pipelined):
    C = chunk
    c_idx = pl.program_id(1)
    wslot = c_idx % 2 if pipelined else 0
    rslot = 1 - wslot if pipelined else 0
    GW = B_WIDTH // B_GROUPS
    HD = B_HEAD_DIM
    PR = range(B_PAIRS)
    cols = lambda t, p: t[:, p * LANES:(p + 1) * LANES]
    F_SKIP, F_GATE, F_ACS = range(3)
    B_XDT, B_XEND = range(2)

    @pl.when(c_idx == 0)
    def _():
        cx_ref[0:CONV_ROWS, :] = cx0_ref[0]
        cbc_ref[0:CONV_ROWS, :] = cbc0_ref[0]
        for p in PR:
            s_ref[:, p * LANES:(p + 1) * LANES] = jnp.concatenate(
                [s0_ref[0, 2 * p].T, s0_ref[0, 2 * p + 1].T], axis=1)
        if pipelined:
            f32_ref[1] = jnp.zeros(f32_ref.shape[1:], F32)
            b16_ref[1] = jnp.zeros(b16_ref.shape[1:], BF16)
            bcs_ref[1] = jnp.zeros(bcs_ref.shape[1:], BF16)
            sdec_ref[1] = jnp.ones(sdec_ref.shape[1:], F32)

    yield
    silu = lambda t: t * _sigmoid(t)

    def causal_conv(raw, ext_ref, taps_ref, bias_ref, sl):
        ext_ref[CONV_ROWS:CONV_ROWS + C, sl] = raw
        acc = bias_ref[:, sl] + taps_ref[CONV_W - 1:CONV_W, sl] * raw
        for j in range(CONV_W - 1):
            start = CONV_ROWS - (CONV_W - 1 - j)
            acc = acc + taps_ref[j:j + 1, sl] * ext_ref[start:start + C, sl]
        ext_ref[0:CONV_ROWS, sl] = raw[C - CONV_ROWS:C, :]
        return silu(acc)

    lane_c = lax.broadcasted_iota(jnp.int32, (C, LANES), 1)
    valid = lane_c < B_HEADS
    dt = jnp.where(valid, jax.nn.softplus(dt_ref[...] + dtb_ref[...]), 0.0)
    d_a = dt * (-jnp.exp(alog_ref[...]))

    ti3 = lax.broadcasted_iota(jnp.int32, (C, 3 * C), 0)
    sj3 = lax.broadcasted_iota(jnp.int32, (C, 3 * C), 1)
    sj3 = jnp.where(sj3 >= 2 * C, sj3 - 2 * C, jnp.where(sj3 >= C, sj3 - C, sj3))
    tri3 = jnp.where(sj3 <= ti3, 1.0, 0.0).astype(BF16)
    a_cs = _dot(tri3, jnp.concatenate(_split3(d_a), axis=0))

    def pack3(t):
        hi, mid, lo = _split3(t)
        return (hi.astype(F32) + pltpu.roll(mid.astype(F32), B_HEADS, axis=1)
                + pltpu.roll(lo.astype(F32), 2 * B_HEADS, axis=1)).astype(BF16)

    expanded = yield jnp.concatenate([pack3(dt), pack3(a_cs)], axis=0)
    dt_e = expanded[0:C]
    acs_new = expanded[C:2 * C]
    alast_new = acs_new[C - 1:C, :]
    f32_ref[wslot, F_ACS] = acs_new
    sdec_ref[wslot] = jnp.exp(alast_new)

    def prepare_bc():
        bcs_ref[wslot] = causal_conv(bc_ref[...], cbc_ref, cwbc_ref, cbbc_ref,
                                     slice(0, BC_WIDTH)).astype(BF16)
        ocbc_ref[0] = cbc_ref[0:CONV_ROWS, :]

    def prepare_tile(t):
        sl = slice(t * LANES, (t + 1) * LANES)
        xs = causal_conv(x_ref[:, sl], cx_ref, cwx_ref, cbx_ref, sl)
        ocx_ref[0, :, sl] = cx_ref[0:CONV_ROWS, sl]
        xdt = xs * dt_e[:, sl]
        b16_ref[wslot, B_XDT, :, sl] = xdt.astype(BF16)
        b16_ref[wslot, B_XEND, :, sl] = (xdt * jnp.exp(alast_new[:, sl] - acs_new[:, sl])).astype(BF16)
        f32_ref[wslot, F_SKIP, :, sl] = xs * dsk_ref[:, sl]
        f32_ref[wslot, F_GATE, :, sl] = silu(z_ref[:, sl])

    pending = [prepare_bc] + [functools.partial(prepare_tile, t) for t in PR]

    def prepare_next():
        if pending:
            pending.pop(0)()

    if not pipelined:
        while pending:
            prepare_next()
            yield
    yield

    ti2 = lax.broadcasted_iota(jnp.int32, (C, 2 * C), 0)
    lane_tt = lax.broadcasted_iota(jnp.int32, (C, 2 * C), 1)
    sj2 = jnp.where(lane_tt >= C, lane_tt - C, lane_tt)
    causal2 = sj2 <= ti2
    eye2 = sj2 == ti2

    acs_e = f32_ref[rslot, F_ACS]
    s_decay = sdec_ref[rslot]
    bcm = bcs_ref[rslot]
    cb2, y_state = [], []
    for g in range(B_GROUPS):
        bm = bcm[:, g * D_STATE:(g + 1) * D_STATE]
        cm = bcm[:, (B_GROUPS + g) * D_STATE:(B_GROUPS + g + 1) * D_STATE]
        cb = lax.dot_general(cm, bm, NT, preferred_element_type=F32)
        cb2.append(jnp.concatenate([cb, cb], axis=1))
        gsl = slice(g * GW, (g + 1) * GW)
        s_old = s_ref[:, gsl]
        y_state.append(_dot(cm, s_old.astype(BF16)) * jnp.exp(acs_e[:, gsl]))
        s_ref[:, gsl] = s_old * s_decay[:, gsl] + lax.dot_general(
            bm, b16_ref[rslot, B_XEND, :, gsl], TN, preferred_element_type=F32)
        prepare_next()
        yield
    y_state = jnp.concatenate(y_state, axis=1)

    ys = []
    for p in PR:
        if C == HD:
            col_form = cols(acs_e, p)
        else:
            col_form = jnp.concatenate([acs_e[:, p * LANES:p * LANES + C],
                                        acs_e[:, p * LANES + HD:p * LANES + HD + C]], axis=1)
        row_form = jnp.sum(jnp.where(eye2, col_form, 0.0), axis=0, keepdims=True)
        m = jnp.where(causal2, jnp.exp(col_form - row_form), 0.0) * cb2[p // (B_PAIRS // B_GROUPS)]
        xdt_p = b16_ref[rslot, B_XDT, :, p * LANES:(p + 1) * LANES]
        ys.append(_dot(m.astype(BF16), _block_diag_rhs(xdt_p, lane_c, HD)))
        prepare_next()
        yield
    while pending:
        prepare_next()
    y = jnp.concatenate(ys, axis=1) + y_state

    y = (y + f32_ref[rslot, F_SKIP]) * f32_ref[rslot, F_GATE]
    outs = []
    for g in range(B_GROUPS):
        yg = y[:, g * GW:(g + 1) * GW]
        ms = jnp.mean(yg * yg, axis=-1, keepdims=True)
        outs.append(yg * lax.rsqrt(ms + EPS))
    yb_ref[...] = (jnp.concatenate(outs, axis=1) * nw_ref[...]).astype(BF16)
    yield MAIN_DONE

    @pl.when(c_idx == pl.num_programs(1) - 1)
    def _():
        for h in range(B_HEADS):
            os_ref[0, h] = s_ref[:, h * HD:(h + 1) * HD].T


N_PROJ_BLOCKS = 4
N_STATE = 3
N_SHARED_RWKV_WTS = 2
N_SHARED_MAMBA_WTS = 1
N_MIXER_OUT = 1 + N_STATE


def _mixer_kernel(*refs, chunk, pipelined, streams, stream_view, shared_state, n_rwkv_wts, n_mamba_wts,
                  n_rwkv_scratch):
    refs = list(refs)
    n_mixer_in = N_PROJ_BLOCKS + N_STATE
    n_rwkv_in = n_mixer_in + n_rwkv_wts
    n_in = n_rwkv_in + n_mixer_in + n_mamba_wts
    outs = refs[n_in:n_in + 2 * N_MIXER_OUT]
    scratch = refs[n_in + 2 * N_MIXER_OUT:]
    one = lambda ref, s: ref.at[pl.ds(s, 1)]

    def mixer_refs(ins, weights, outs, scratch, s):
        proj = [stream_view(r, s) for r in ins[:N_PROJ_BLOCKS]]
        state = [r if shared_state else one(r, s) for r in ins[N_PROJ_BLOCKS:n_mixer_in]]
        return (proj + state + weights + [stream_view(outs[0], s)]
                + [one(r, s) for r in outs[1:]] + [r.at[s] for r in scratch])

    rwkv_ins = refs[:n_rwkv_in]
    wa_up_ref, g_up_ref = rwkv_ins[n_mixer_in:n_mixer_in + N_SHARED_RWKV_WTS]
    rwkv_programs = [_rwkv_program(
        *mixer_refs(rwkv_ins, rwkv_ins[n_mixer_in + N_SHARED_RWKV_WTS:], outs[:N_MIXER_OUT],
                    scratch[:n_rwkv_scratch], s), chunk=chunk, pipelined=pipelined)
        for s in range(streams)]
    mamba_ins = refs[n_rwkv_in:n_in]
    (e3_ref,) = mamba_ins[n_mixer_in:n_mixer_in + N_SHARED_MAMBA_WTS]
    mamba_programs = [_mamba_program(
        *mixer_refs(mamba_ins, mamba_ins[n_mixer_in + N_SHARED_MAMBA_WTS:], outs[N_MIXER_OUT:],
                    scratch[n_rwkv_scratch:], s), chunk=chunk, pipelined=pipelined)
        for s in range(streams)]
    programs = [p for pair in zip(rwkv_programs, mamba_programs) for p in pair]
    for prog in programs:
        next(prog)
    l1, gl = zip(*[next(prog) for prog in rwkv_programs])
    wa = _dot(jnp.concatenate(l1, axis=0), wa_up_ref[...])
    gate = _dot(jnp.concatenate(gl, axis=0), g_up_ref[...])
    packed = [next(prog) for prog in mamba_programs]
    expanded = _dot(jnp.concatenate(packed, axis=0), e3_ref[...])
    for s in range(streams):
        rows = slice(s * chunk, (s + 1) * chunk)
        rwkv_programs[s].send((wa[rows], gate[rows]))
        mamba_programs[s].send(expanded[2 * s * chunk:2 * (s + 1) * chunk])
    main_done = [False] * len(programs)
    while not all(main_done):
        for i, prog in enumerate(programs):
            if not main_done[i]:
                main_done[i] = next(prog) is MAIN_DONE
    for prog in programs:
        for _ in prog:
            pass


def _mixers(proj, row_off, nb, nc, chunk, rwkv_state, mamba_state, shared_state, rwkv_wts, mamba_wts):
    C = chunk
    assert CONV_ROWS <= C <= B_HEAD_DIM
    W = A_WIDTH
    S = 2 if nb % 2 == 0 else 1
    pipelined, in_chunk, out_chunk = _chunk_pipeline(nc)
    slots = 2 if pipelined else 1
    if nc == 1:
        assert row_off % (S * C) == 0
        rb = row_off // (S * C)
        zspec = lambda width, col: pl.BlockSpec((S * C, width), lambda b, c: (rb + b, col // width))
        yspec = pl.BlockSpec((S * C, W), lambda b, c: (b, 0))
        y_shape = jax.ShapeDtypeStruct((nb * C, W), BF16)
        stream_view = lambda ref, s: ref.at[pl.ds(s * C, C)]
    else:
        assert row_off == 0 and proj.shape[0] == nb * nc * C
        proj = proj.reshape(nb, nc * C, PROJ_COLS)
        zspec = lambda width, col: pl.BlockSpec(
            (S, C, width), lambda b, c: (b, in_chunk(c), col // width))
        yspec = pl.BlockSpec((S, C, W), lambda b, c: (b, out_chunk(c), 0))
        y_shape = jax.ShapeDtypeStruct((nb, nc * C, W), BF16)
        stream_view = lambda ref, s: ref.at[s]
    full = lambda arr: pl.BlockSpec(arr.shape, lambda b, c: (0,) * arr.ndim)

    def state_specs(blocks):
        zeros = lambda blk: (0,) * (len(blk) - 1)
        if shared_state:
            ins = [pl.BlockSpec(blk, lambda b, c, z=zeros(blk): (0,) + z) for blk in blocks]
        else:
            ins = [pl.BlockSpec((S,) + blk[1:], lambda b, c, z=zeros(blk): (b,) + z) for blk in blocks]
        outs = [pl.BlockSpec((S,) + blk[1:], lambda b, c, z=zeros(blk): (b,) + z) for blk in blocks]
        shapes = [jax.ShapeDtypeStruct((nb,) + blk[1:], F32) for blk in blocks]
        return ins, outs, shapes

    r_in, r_out, r_shapes = state_specs(
        [(1, 1, 3 * W), (1, 1, LR_PAD), (1, A_HEADS, A_HEAD_DIM, A_HEAD_DIM)])
    m_in, m_out, m_shapes = state_specs(
        [(1, CONV_ROWS, B_WIDTH), (1, CONV_ROWS, BC_WIDTH), (1, B_HEADS, B_HEAD_DIM, D_STATE)])
    rwkv_in = [zspec(W, COL_R), zspec(W, COL_K), zspec(W, COL_V), zspec(LR_PAD, COL_LR)] + r_in \
        + [full(w) for w in rwkv_wts]
    mamba_in = [zspec(B_WIDTH, COL_Z), zspec(B_WIDTH, COL_X), zspec(BC_WIDTH, COL_BC),
                zspec(DT_PAD, COL_DT)] + m_in + [full(w) for w in mamba_wts]
    per_stream = lambda shape, dtype: pltpu.VMEM((S,) + shape, dtype)
    rwkv_scratch = [
        per_stream((1, 3 * W + LR_PAD), F32),
        per_stream((A_PAIRS, LANES, LANES), F32),
        per_stream((slots, 7, C, W), BF16),
        per_stream((slots, 2, C, W), F32),
        per_stream((slots, 1, W), F32),
    ]
    mamba_scratch = [
        per_stream((CONV_ROWS + C, B_WIDTH), F32),
        per_stream((CONV_ROWS + C, BC_WIDTH), F32),
        per_stream((D_STATE, B_WIDTH), F32),
        per_stream((slots, 3, C, B_WIDTH), F32),
        per_stream((slots, 2, C, B_WIDTH), BF16),
        per_stream((slots, C, BC_WIDTH), BF16),
        per_stream((slots, 1, B_WIDTH), F32),
    ]
    outs = pl.pallas_call(
        functools.partial(_mixer_kernel, chunk=C, pipelined=pipelined, streams=S,
                          stream_view=stream_view, shared_state=shared_state,
                          n_rwkv_wts=len(rwkv_wts), n_mamba_wts=len(mamba_wts),
                          n_rwkv_scratch=len(rwkv_scratch)),
        grid=(nb // S, nc + 1 if pipelined else nc),
        in_specs=rwkv_in + mamba_in,
        out_specs=[yspec] + r_out + [yspec] + m_out,
        out_shape=[y_shape] + r_shapes + [y_shape] + m_shapes,
        scratch_shapes=rwkv_scratch + mamba_scratch,
        compiler_params=_cparams(("parallel", "arbitrary")),
        name="mixers",
    )(*[proj] * N_PROJ_BLOCKS, *rwkv_state, *rwkv_wts, *[proj] * N_PROJ_BLOCKS, *mamba_state, *mamba_wts)
    outs = [o.reshape(nb * nc * C, W) if i % N_MIXER_OUT == 0 else o for i, o in enumerate(outs)]
    return outs[:N_MIXER_OUT], outs[N_MIXER_OUT:]


def _pad_cols(a, width):
    return jnp.pad(a, [(0, 0)] * (a.ndim - 1) + [(0, width - a.shape[-1])])


def kernel(x_prompt, x_sample, state_rwkv_shift, state_rwkv_wkv, state_ssm_conv, state_ssm, meta_tokens, norm_mix_w, w_in, rwkv_mu, rwkv_w0, rwkv_w_up, rwkv_a0, rwkv_a_up, rwkv_g_up, rwkv_k_k, rwkv_k_a, rwkv_r_k, rwkv_lnx_w, rwkv_lnx_b, ssm_conv_w, ssm_conv_b, ssm_dt_bias, ssm_A_log, ssm_D, ssm_norm_w, w_out, norm_ffn_w, ffn_w_gate, ffn_w_up, ffn_w_down, norm_final_w):
    depth = w_in.shape[0]
    assert depth == 1, "single-layer step"
    bp, lp, d = x_prompt.shape
    bs, ls, _ = x_sample.shape
    n_meta = meta_tokens.shape[0]
    assert d == D_MODEL and lp % CHUNK == 0 and ls <= CHUNK and n_meta <= CHUNK
    assert w_in.shape[-1] == A_COLS + 2 * B_WIDTH + BC_WIDTH + B_HEADS
    W = A_WIDTH

    wi = w_in[0]
    w_perm = jnp.concatenate([
        wi[:, 0:3 * W],
        wi[:, A_COLS:A_COLS + 2 * B_WIDTH + BC_WIDTH],
        _pad_cols(wi[:, 3 * W:A_COLS], LR_PAD),
        _pad_cols(wi[:, A_COLS + 2 * B_WIDTH + BC_WIDTH:], DT_COLS),
    ], axis=1).astype(BF16)
    nmw = norm_mix_w.reshape(1, D_MODEL)
    mu = rwkv_mu[0]
    mu_rkv = mu[None, 0:3 * W]
    mu_l = _pad_cols(mu[None, 3 * W:], LR_PAD)
    wa_up = jnp.zeros((LANES, 2 * W), F32)
    wa_up = wa_up.at[0:DECAY_RANK, 0:W].set(rwkv_w_up[0])
    wa_up = wa_up.at[DECAY_RANK:DECAY_RANK + AAA_RANK, W:].set(rwkv_a_up[0]).astype(BF16)
    g_up = jnp.pad(rwkv_g_up[0], ((0, 2 * LANES - GATE_RANK), (0, 0))).astype(BF16)
    lane_head = jnp.arange(LANES, dtype=jnp.int32) // A_HEAD_DIM
    ones_bd = (lane_head[:, None] == lane_head[None, :]).astype(BF16)
    rwkv_wts = (wa_up, g_up, mu_rkv, mu_l, rwkv_w0.reshape(1, W), rwkv_a0.reshape(1, W),
                rwkv_k_k.reshape(1, W), rwkv_k_a.reshape(1, W), rwkv_r_k.reshape(1, W),
                rwkv_lnx_w.reshape(1, W), rwkv_lnx_b.reshape(1, W), ones_bd)
    taps = ssm_conv_w[0].T
    cbias = ssm_conv_b.reshape(1, -1)
    src_head = jnp.where(jnp.arange(LANES) < 3 * B_HEADS, jnp.arange(LANES) % B_HEADS, -1)
    e3 = (src_head[:, None] == (jnp.arange(B_WIDTH) // B_HEAD_DIM)[None, :]).astype(BF16)
    mamba_wts = (e3, taps[:, :B_WIDTH], taps[:, B_WIDTH:], cbias[:, :B_WIDTH], cbias[:, B_WIDTH:],
                 _pad_cols(ssm_dt_bias.reshape(1, B_HEADS), DT_PAD),
                 _pad_cols(ssm_A_log.reshape(1, B_HEADS), DT_PAD),
                 jnp.repeat(ssm_D.reshape(1, B_HEADS), B_HEAD_DIM, axis=1),
                 ssm_norm_w.reshape(1, B_WIDTH))
    wo = w_out[0].astype(BF16)
    wo_a, wo_b = wo[:W], wo[W:]
    nfw = norm_ffn_w.reshape(1, D_MODEL)
    wg = ffn_w_gate[0].astype(BF16)
    wu = ffn_w_up[0].astype(BF16)
    wd = ffn_w_down[0].astype(BF16)
    nlw = norm_final_w.reshape(1, D_MODEL)

    tm_small = 384
    n_small = bs * ls + n_meta
    m_small = -(-n_small // tm_small) * tm_small
    x_small = jnp.concatenate([x_sample.reshape(bs * ls, d), meta_tokens,
                               jnp.zeros((m_small - n_small, d), F32)], axis=0)
    x_big = x_prompt.reshape(bp * lp, d)
    tm_big = math.gcd(bp * lp, 512)

    proj_small = _proj(x_small, nmw, w_perm, m_small)
    proj_big = _proj(x_big, nmw, w_perm, math.gcd(bp * lp, 1024))

    zeros = lambda *s: jnp.zeros(s, F32)
    sh = state_rwkv_shift[0]
    smp_rwkv = (sh[:, None, 0:3 * W], _pad_cols(sh[:, None, 3 * W:], LR_PAD), state_rwkv_wkv[0])
    conv0 = jnp.pad(state_ssm_conv[0], ((0, 0), (CONV_ROWS - (CONV_W - 1), 0), (0, 0)))
    smp_mamba = (conv0[..., :B_WIDTH], conv0[..., B_WIDTH:], state_ssm[0])
    meta_rwkv0 = (zeros(1, 1, 3 * W), zeros(1, 1, LR_PAD), zeros(1, A_HEADS, A_HEAD_DIM, A_HEAD_DIM))
    meta_mamba0 = (zeros(1, CONV_ROWS, B_WIDTH), zeros(1, CONV_ROWS, BC_WIDTH),
                   zeros(1, B_HEADS, B_HEAD_DIM, D_STATE))

    (ya_m, *meta_rwkv), (yb_m, *meta_mamba) = _mixers(
        proj_small, bs * ls, 1, 1, n_meta, meta_rwkv0, meta_mamba0, False, rwkv_wts, mamba_wts)
    (ya_s, *s_rwkv), (yb_s, *s_mamba) = _mixers(
        proj_small, 0, bs, 1, ls, smp_rwkv, smp_mamba, False, rwkv_wts, mamba_wts)
    (ya_p, *p_rwkv), (yb_p, *p_mamba) = _mixers(
        proj_big, 0, bp, lp // CHUNK, CHUNK, meta_rwkv, meta_mamba, True, rwkv_wts, mamba_wts)

    def tail(ya, yb, x, tm, tm_ffn1):
        x1, h2 = _outproj(x, ya, yb, wo_a, wo_b, nfw, tm)
        g = _ffn1(h2, wg, wu, tm_ffn1, 512)
        return _ffn2(g, wd, x1, nlw, tm, D_FF // 2)

    pad_rows = lambda *parts: jnp.concatenate(
        list(parts) + [jnp.zeros((m_small - n_small, parts[0].shape[1]), parts[0].dtype)], axis=0)
    y_small = tail(pad_rows(ya_s, ya_m), pad_rows(yb_s, yb_m), x_small, tm_small, m_small)
    y_big = tail(ya_p, yb_p, x_big, tm_big, math.gcd(bp * lp, 1024))

    def states_out(rw, mb):
        sh_rkv, sh_l, wkv = rw
        cx, cbc, ssm = mb
        shift = jnp.concatenate([sh_rkv[:, 0], sh_l[:, 0, :LOWRANK]], axis=-1)[None]
        conv = jnp.concatenate([cx, cbc], axis=-1)[None, :, CONV_ROWS - (CONV_W - 1):]
        return shift, wkv[None], conv, ssm[None]

    y_prompt = y_big.reshape(bp, lp, d)
    y_sample = y_small[:bs * ls].reshape(bs, ls, d)
    return (y_prompt, y_sample) + states_out(p_rwkv, p_mamba) + states_out(s_rwkv, s_mamba)
```

```python
import functools
import math

import jax
import jax.numpy as jnp
from jax import lax
from jax.experimental import pallas as pl
from jax.experimental.pallas import tpu as pltpu

F32 = jnp.float32
BF16 = jnp.bfloat16

D_MODEL = 2048
CHUNK = 64
EPS = 1e-6
LNX_EPS = 64e-5
A_WIDTH = 1024
A_HEADS = 16
A_HEAD_DIM = 64
A_PAIRS = A_HEADS // 2
DECAY_RANK = 64
AAA_RANK = 64
GATE_RANK = 160
B_WIDTH = 1024
B_HEADS = 16
B_HEAD_DIM = 64
B_PAIRS = B_HEADS // 2
B_GROUPS = 2
D_STATE = 128
CONV_W = 4
BC_WIDTH = 2 * B_GROUPS * D_STATE
A_COLS = 3 * A_WIDTH + DECAY_RANK + AAA_RANK + GATE_RANK
LOWRANK = DECAY_RANK + AAA_RANK + GATE_RANK
D_FF = 5632

LANES = 128
SUBLANES = 8
VMEM_LIMIT = 56 * 1024 * 1024

LR_PAD = 512
DT_PAD = 128
DT_COLS = 256
COL_R, COL_K, COL_V, COL_Z, COL_X = 0, 1024, 2048, 3072, 4096
COL_BC = 5120
COL_LR = 5632
COL_DT = 6144
PROJ_COLS = COL_DT + DT_COLS
PROJ_TN = 1280
CONV_ROWS = SUBLANES

NT = (((1,), (1,)), ((), ()))
TN = (((0,), (0,)), ((), ()))


def _cparams(sem):
    return pltpu.CompilerParams(dimension_semantics=sem, vmem_limit_bytes=VMEM_LIMIT)


def _dot(x, y):
    return jnp.dot(x, y, preferred_element_type=F32)


def _block_diag_rhs(x, lane, split):
    zero = jnp.zeros_like(x)
    return jnp.concatenate([jnp.where(lane < split, x, zero), jnp.where(lane < split, zero, x)], axis=0)


MAIN_DONE = "main done"
REQ_LOWRANK, REQ_EXPAND, REQ_HEAD_SUMS = "low-rank up-projections", "head expansion", "head sums"


def _sigmoid(t):
    return 0.5 + 0.5 * jnp.tanh(0.5 * t)


def _chunk_pipeline(nc):
    if nc > 1:
        return True, (lambda c: jnp.minimum(c, nc - 1)), (lambda c: jnp.maximum(c - 1, 0))
    return False, (lambda c: c), (lambda c: c)


def _split3(x):
    hi = x.astype(BF16)
    r1 = x - hi.astype(F32)
    mid = r1.astype(BF16)
    return hi, mid, (r1 - mid.astype(F32)).astype(BF16)


def _proj_kernel(x_ref, nw_ref, w_ref, o_ref, h_ref):
    @pl.when(pl.program_id(1) == 0)
    def _():
        x = x_ref[...]
        ms = jnp.mean(x * x, axis=-1, keepdims=True)
        h_ref[...] = (x * lax.rsqrt(ms + EPS) * nw_ref[...]).astype(BF16)

    o_ref[...] = _dot(h_ref[...], w_ref[...])


def _proj(x, norm_w, w, tm):
    m = x.shape[0]
    return pl.pallas_call(
        _proj_kernel,
        grid=(m // tm, PROJ_COLS // PROJ_TN),
        in_specs=[
            pl.BlockSpec((tm, D_MODEL), lambda i, j: (i, 0)),
            pl.BlockSpec((1, D_MODEL), lambda i, j: (0, 0)),
            pl.BlockSpec((D_MODEL, PROJ_TN), lambda i, j: (0, j)),
        ],
        out_specs=pl.BlockSpec((tm, PROJ_TN), lambda i, j: (i, j)),
        out_shape=jax.ShapeDtypeStruct((m, PROJ_COLS), F32),
        scratch_shapes=[pltpu.VMEM((tm, D_MODEL), BF16)],
        compiler_params=_cparams(("parallel", "arbitrary")),
        name="proj",
    )(x, norm_w, w)


def _outproj_kernel(x_ref, ya_ref, yb_ref, wa_ref, wb_ref, nw_ref, x1_ref, h2_ref):
    x1 = x_ref[...] + _dot(ya_ref[...], wa_ref[...]) + _dot(yb_ref[...], wb_ref[...])
    x1_ref[...] = x1
    ms = jnp.mean(x1 * x1, axis=-1, keepdims=True)
    h2_ref[...] = (x1 * lax.rsqrt(ms + EPS) * nw_ref[...]).astype(BF16)


def _outproj(x, ya, yb, wa, wb, norm_w, tm):
    m = x.shape[0]
    row = lambda i: (i, 0)
    fixed = lambda i: (0, 0)
    return pl.pallas_call(
        _outproj_kernel,
        grid=(m // tm,),
        in_specs=[
            pl.BlockSpec((tm, D_MODEL), row),
            pl.BlockSpec((tm, A_WIDTH), row),
            pl.BlockSpec((tm, B_WIDTH), row),
            pl.BlockSpec((A_WIDTH, D_MODEL), fixed),
            pl.BlockSpec((B_WIDTH, D_MODEL), fixed),
            pl.BlockSpec((1, D_MODEL), fixed),
        ],
        out_specs=[pl.BlockSpec((tm, D_MODEL), row), pl.BlockSpec((tm, D_MODEL), row)],
        out_shape=[jax.ShapeDtypeStruct((m, D_MODEL), F32),
                   jax.ShapeDtypeStruct((m, D_MODEL), BF16)],
        compiler_params=_cparams(("parallel",)),
        name="outproj",
    )(x, ya, yb, wa, wb, norm_w)


def _ffn1_kernel(h_ref, wg_ref, wu_ref, o_ref):
    h = h_ref[...]
    g = _dot(h, wg_ref[...])
    u = _dot(h, wu_ref[...])
    o_ref[...] = (g * jax.nn.sigmoid(g) * u).astype(BF16)


def _ffn1(h2, wg, wu, tm, tn):
    m = h2.shape[0]
    return pl.pallas_call(
        _ffn1_kernel,
        grid=(m // tm, D_FF // tn),
        in_specs=[
            pl.BlockSpec((tm, D_MODEL), lambda i, j: (i, 0)),
            pl.BlockSpec((D_MODEL, tn), lambda i, j: (0, j)),
            pl.BlockSpec((D_MODEL, tn), lambda i, j: (0, j)),
        ],
        out_specs=pl.BlockSpec((tm, tn), lambda i, j: (i, j)),
        out_shape=jax.ShapeDtypeStruct((m, D_FF), BF16),
        compiler_params=_cparams(("parallel", "parallel")),
        name="ffn1",
    )(h2, wg, wu)


def _ffn2_kernel(g_ref, wd_ref, x1_ref, nw_ref, o_ref):
    k = pl.program_id(1)

    @pl.when(k == 0)
    def _():
        o_ref[...] = x1_ref[...] + _dot(g_ref[...], wd_ref[...])

    @pl.when(jnp.logical_and(k > 0, k < pl.num_programs(1) - 1))
    def _():
        o_ref[...] += _dot(g_ref[...], wd_ref[...])

    @pl.when(k == pl.num_programs(1) - 1)
    def _():
        x2 = o_ref[...] + _dot(g_ref[...], wd_ref[...])
        ms = jnp.mean(x2 * x2, axis=-1, keepdims=True)
        o_ref[...] = x2 * lax.rsqrt(ms + EPS) * nw_ref[...]


def _ffn2(g, wd, x1, norm_w, tm, tk):
    m = g.shape[0]
    assert D_FF // tk >= 2
    return pl.pallas_call(
        _ffn2_kernel,
        grid=(m // tm, D_FF // tk),
        in_specs=[
            pl.BlockSpec((tm, tk), lambda i, k: (i, k)),
            pl.BlockSpec((tk, D_MODEL), lambda i, k: (k, 0)),
            pl.BlockSpec((tm, D_MODEL), lambda i, k: (i, 0)),
            pl.BlockSpec((1, D_MODEL), lambda i, k: (0, 0)),
        ],
        out_specs=pl.BlockSpec((tm, D_MODEL), lambda i, k: (i, 0)),
        out_shape=jax.ShapeDtypeStruct((m, D_MODEL), F32),
        compiler_params=_cparams(("parallel", "arbitrary")),
        name="ffn2",
    )(g, wd, x1, norm_w)


def _shift_rows(z, prev_row):
    rolled = pltpu.roll(z, 1, axis=0)
    row = lax.broadcasted_iota(jnp.int32, z.shape, 0)
    return jnp.where(row == 0, prev_row, rolled)


def _rwkv_program(zr_ref, zk_ref, zv_ref, zl_ref, sh_rkv_ref, sh_l_ref, s0_ref,
                 mu_rkv_ref, mu_l_ref, w0_ref, a0_ref,
                 kk_ref, ka_ref, rk_ref, lnw_ref, lnb_ref,
                 ya_ref, osh_rkv_ref, osh_l_ref, os_ref,
                 prev_ref, s_ref, ops_ref, aux_ref, plast_ref, *, chunk, pipelined):
    C = chunk
    W = A_WIDTH
    HD = A_HEAD_DIM
    c_idx = pl.program_id(1)
    wslot = c_idx % 2 if pipelined else 0
    rslot = 1 - wslot if pipelined else 0
    PR = range(A_PAIRS)
    cols = lambda t, p: t[:, p * LANES:(p + 1) * LANES]
    OP_AT, OP_RT, OP_KH, OP_BH, OP_KE, OP_BE, OP_V = range(7)
    AUX_BONUS, AUX_GATE = range(2)

    @pl.when(c_idx == 0)
    def _():
        prev_ref[:, 0:3 * W] = sh_rkv_ref[0]
        prev_ref[:, 3 * W:3 * W + LR_PAD] = sh_l_ref[0]
        zero = jnp.zeros((HD, HD), F32)
        for p in PR:
            s_ref[p] = jnp.concatenate(
                [jnp.concatenate([s0_ref[0, 2 * p], zero], axis=1),
                 jnp.concatenate([zero, s0_ref[0, 2 * p + 1]], axis=1)], axis=0)
        if pipelined:
            ops_ref[1] = jnp.zeros(ops_ref.shape[1:], BF16)
            aux_ref[1] = jnp.zeros(aux_ref.shape[1:], F32)
            plast_ref[1] = jnp.ones(plast_ref.shape[1:], F32)

    yield
    ti2 = lax.broadcasted_iota(jnp.int32, (C, 2 * C), 0)
    lane_tt = lax.broadcasted_iota(jnp.int32, (C, 2 * C), 1)
    sj2 = jnp.where(lane_tt >= C, lane_tt - C, lane_tt)
    incl2 = sj2 <= ti2
    strict2 = sj2 < ti2
    tri2_b = jnp.where(incl2, 1.0, 0.0).astype(BF16)
    lane_c = lax.broadcasted_iota(jnp.int32, (C, LANES), 1)
    bd_i = lax.broadcasted_iota(jnp.int32, (LANES, LANES), 0) // HD
    bd_j = lax.broadcasted_iota(jnp.int32, (LANES, LANES), 1) // HD
    bd_mask = bd_i == bd_j
    n_sq = int(math.log2(C)) - 1
    bd_ch = lambda x: _block_diag_rhs(x, lane_c, HD)
    bd_tt = lambda x: _block_diag_rhs(x, lane_tt, C)

    def mixed(z, prev_lo, mu):
        width = z.shape[1]
        zp = _shift_rows(z, prev_ref[:, prev_lo:prev_lo + width])
        prev_ref[:, prev_lo:prev_lo + width] = z[C - 1:C, :]
        return z + mu * (zp - z)

    zl_raw = zl_ref[...]
    zl = mixed(zl_raw, 3 * W, mu_l_ref[...])
    osh_l_ref[0] = zl_raw[C - 1:C, :]
    l1 = zl[:, 0:LANES]
    l1 = jnp.where(lane_c < DECAY_RANK, jnp.tanh(l1), l1)
    gl = _sigmoid(zl[:, LANES:3 * LANES])
    wa, gate = yield REQ_LOWRANK, (l1.astype(BF16), gl.astype(BF16))
    aux_ref[wslot, AUX_GATE] = gate

    def hsum_tiles(tiles):
        s = yield REQ_HEAD_SUMS, jnp.concatenate(tiles, axis=0).astype(BF16)
        return [s[i * C:(i + 1) * C] for i in range(len(tiles))]

    def prepare_tile(p):
        lo, hi = p * LANES, (p + 1) * LANES
        sl = slice(lo, hi)

        def mixed_rkv(z_ref, base):
            z = z_ref[:, sl]
            osh_rkv_ref[0, :, base + lo:base + hi] = z[C - 1:C, :]
            return mixed(z, base + lo, mu_rkv_ref[:, base + lo:base + hi])

        r = mixed_rkv(zr_ref, 0)
        k = mixed_rkv(zk_ref, W)
        v = mixed_rkv(zv_ref, 2 * W)
        wlog = -jax.nn.softplus(-(w0_ref[:, sl] + wa[:, sl])) - 0.5
        lw = -jnp.exp(wlog)
        a = _sigmoid(a0_ref[:, sl] + wa[:, W + lo:W + hi])
        kk = k * kk_ref[:, sl]
        k2 = k * (1.0 + (a - 1.0) * ka_ref[:, sl])
        (ss,) = yield from hsum_tiles([kk * kk])
        kk = kk * lax.rsqrt(jnp.maximum(ss, 1e-24))
        ka = kk * a
        lw_hi = lw.astype(BF16)
        lw_lo = (lw - lw_hi.astype(F32)).astype(BF16)
        cum = _dot(tri2_b, jnp.concatenate([lw_hi, lw_lo], axis=0))
        cum_last = cum[C - 1:C, :]
        e_neg = jnp.exp(-cum)
        e_end = jnp.exp(cum_last - cum)
        ops_ref[wslot, OP_AT, :, sl] = (kk * jnp.exp(cum - lw)).astype(BF16)
        ops_ref[wslot, OP_RT, :, sl] = (r * jnp.exp(cum)).astype(BF16)
        ops_ref[wslot, OP_KH, :, sl] = (k2 * e_neg).astype(BF16)
        ops_ref[wslot, OP_BH, :, sl] = (ka * e_neg).astype(BF16)
        ops_ref[wslot, OP_KE, :, sl] = (k2 * e_end).astype(BF16)
        ops_ref[wslot, OP_BE, :, sl] = (-ka * e_end).astype(BF16)
        ops_ref[wslot, OP_V, :, sl] = v.astype(BF16)
        plast_ref[wslot, :, sl] = jnp.exp(cum_last)
        (rk,) = yield from hsum_tiles([r * k2 * rk_ref[:, sl]])
        aux_ref[wslot, AUX_BONUS, :, sl] = rk * v

    pending = list(PR)

    def prepare_next():
        if pending:
            yield from prepare_tile(pending.pop(0))

    if not pipelined:
        while pending:
            yield from prepare_next()

    op = lambda i, p: ops_ref[rslot, i, :, p * LANES:(p + 1) * LANES]
    s_old = [s_ref[p] for p in PR]
    v_b = [op(OP_V, p) for p in PR]
    lhs = [jnp.concatenate([op(OP_AT, p), op(OP_RT, p)], axis=0) for p in PR]
    rhs = [jnp.concatenate([bd_ch(op(OP_KH, p)), bd_ch(op(OP_BH, p))], axis=0) for p in PR]
    upd_rhs = [jnp.concatenate([op(OP_KE, p), op(OP_BE, p)], axis=0) for p in PR]
    p_last = plast_ref[rslot]
    m1 = [lax.dot_general(lhs[p], rhs[p], NT, preferred_element_type=F32) for p in PR]
    m2 = [lax.dot_general(lhs[p], s_old[p].astype(BF16), NT, preferred_element_type=F32) for p in PR]
    yield from prepare_next()
    yield
    a_k = [jnp.where(strict2, m1[p][0:C, 0:2 * C], 0.0).astype(BF16) for p in PR]
    a_b = [jnp.where(strict2, m1[p][0:C, 2 * C:4 * C], 0.0) for p in PR]
    b_kb = [jnp.concatenate([jnp.where(incl2, m1[p][C:2 * C, 0:2 * C], 0.0),
                             jnp.where(incl2, -m1[p][C:2 * C, 2 * C:4 * C], 0.0)], axis=1).astype(BF16)
            for p in PR]
    bd_v = [bd_ch(v_b[p]) for p in PR]
    u = [m2[p][0:C] + _dot(a_k[p], bd_v[p]) for p in PR]
    yield from prepare_next()
    yield
    pw_b = [a_b[p].astype(BF16) for p in PR]
    u = [u[p] - _dot(pw_b[p], bd_ch(u[p].astype(BF16))) for p in PR]
    yield from prepare_next()
    yield
    for _ in range(n_sq):
        pw_b = [_dot(pw_b[p], bd_tt(pw_b[p])).astype(BF16) for p in PR]
        u = [u[p] + _dot(pw_b[p], bd_ch(u[p].astype(BF16))) for p in PR]
        yield from prepare_next()
        yield
    u_b = [u[p].astype(BF16) for p in PR]
    ys = [m2[p][C:2 * C] + _dot(b_kb[p], jnp.concatenate([bd_v[p], bd_ch(u_b[p])], axis=0)) for p in PR]
    upd = [lax.dot_general(jnp.concatenate([v_b[p], u_b[p]], axis=0), upd_rhs[p],
                           TN, preferred_element_type=F32) for p in PR]
    for p in PR:
        s_ref[p] = s_old[p] * cols(p_last, p) + jnp.where(bd_mask, upd[p], 0.0)
    while pending:
        yield from prepare_next()
    yield

    inv_n = 1.0 / A_HEAD_DIM
    mean = yield from hsum_tiles(ys)
    d = [ys[p] - mean[p] * inv_n for p in PR]
    var = yield from hsum_tiles([d[p] * d[p] for p in PR])
    yn = jnp.concatenate([d[p] * lax.rsqrt(var[p] * inv_n + LNX_EPS) for p in PR], axis=1)
    y = yn * lnw_ref[...] + lnb_ref[...] + aux_ref[rslot, AUX_BONUS]
    ya_ref[...] = (y * aux_ref[rslot, AUX_GATE]).astype(BF16)
    yield MAIN_DONE

    @pl.when(c_idx == pl.num_programs(1) - 1)
    def _():
        for p in PR:
            s_pair = s_ref[p]
            os_ref[0, 2 * p] = s_pair[0:HD, 0:HD]
            os_ref[0, 2 * p + 1] = s_pair[HD:2 * HD, HD:2 * HD]


def _mamba_program(z_ref, x_ref, bc_ref, dt_ref, cx0_ref, cbc0_ref, s0_ref,
                  cwx_ref, cwbc_ref, cbx_ref, cbbc_ref, dtb_ref, alog_ref, dsk_ref, nw_ref,
                  yb_ref, ocx_ref, ocbc_ref, os_ref,
                  cx_ref, cbc_ref, s_ref, f32_ref, b16_ref, bcs_ref, sdec_ref, *, chunk, pipelined):
    C = chunk
    c_idx = pl.program_id(1)
    wslot = c_idx % 2 if pipelined else 0
    rslot = 1 - wslot if pipelined else 0
    GW = B_WIDTH // B_GROUPS
    HD = B_HEAD_DIM
    PR = range(B_PAIRS)
    cols = lambda t, p: t[:, p * LANES:(p + 1) * LANES]
    F_SKIP, F_GATE, F_ACS = range(3)
    B_XDT, B_XEND = range(2)

    @pl.when(c_idx == 0)
    def _():
        cx_ref[0:CONV_ROWS, :] = cx0_ref[0]
        cbc_ref[0:CONV_ROWS, :] = cbc0_ref[0]
        for p in PR:
            s_ref[:, p * LANES:(p + 1) * LANES] = jnp.concatenate(
                [s0_ref[0, 2 * p].T, s0_ref[0, 2 * p + 1].T], axis=1)
        if pipelined:
            f32_ref[1] = jnp.zeros(f32_ref.shape[1:], F32)
            b16_ref[1] = jnp.zeros(b16_ref.shape[1:], BF16)
            bcs_ref[1] = jnp.zeros(bcs_ref.shape[1:], BF16)
            sdec_ref[1] = jnp.ones(sdec_ref.shape[1:], F32)

    yield
    silu = lambda t: t * _sigmoid(t)

    def causal_conv(raw, ext_ref, taps_ref, bias_ref, sl):
        ext_ref[CONV_ROWS:CONV_ROWS + C, sl] = raw
        acc = bias_ref[:, sl] + taps_ref[CONV_W - 1:CONV_W, sl] * raw
        for j in range(CONV_W - 1):
            start = CONV_ROWS - (CONV_W - 1 - j)
            acc = acc + taps_ref[j:j + 1, sl] * ext_ref[start:start + C, sl]
        ext_ref[0:CONV_ROWS, sl] = raw[C - CONV_ROWS:C, :]
        return silu(acc)

    lane_c = lax.broadcasted_iota(jnp.int32, (C, LANES), 1)
    valid = lane_c < B_HEADS
    dt = jnp.where(valid, jax.nn.softplus(dt_ref[...] + dtb_ref[...]), 0.0)
    d_a = dt * (-jnp.exp(alog_ref[...]))

    ti3 = lax.broadcasted_iota(jnp.int32, (C, 3 * C), 0)
    sj3 = lax.broadcasted_iota(jnp.int32, (C, 3 * C), 1)
    sj3 = jnp.where(sj3 >= 2 * C, sj3 - 2 * C, jnp.where(sj3 >= C, sj3 - C, sj3))
    tri3 = jnp.where(sj3 <= ti3, 1.0, 0.0).astype(BF16)
    a_cs = _dot(tri3, jnp.concatenate(_split3(d_a), axis=0))

    def pack3(t):
        hi, mid, lo = _split3(t)
        return (hi.astype(F32) + pltpu.roll(mid.astype(F32), B_HEADS, axis=1)
                + pltpu.roll(lo.astype(F32), 2 * B_HEADS, axis=1)).astype(BF16)

    expanded = yield REQ_EXPAND, jnp.concatenate([pack3(dt), pack3(a_cs)], axis=0)
    dt_e = expanded[0:C]
    acs_new = expanded[C:2 * C]
    alast_new = acs_new[C - 1:C, :]
    f32_ref[wslot, F_ACS] = acs_new
    sdec_ref[wslot] = jnp.exp(alast_new)

    def prepare_bc():
        bcs_ref[wslot] = causal_conv(bc_ref[...], cbc_ref, cwbc_ref, cbbc_ref,
                                     slice(0, BC_WIDTH)).astype(BF16)
        ocbc_ref[0] = cbc_ref[0:CONV_ROWS, :]

    def prepare_tile(t):
        sl = slice(t * LANES, (t + 1) * LANES)
        xs = causal_conv(x_ref[:, sl], cx_ref, cwx_ref, cbx_ref, sl)
        ocx_ref[0, :, sl] = cx_ref[0:CONV_ROWS, sl]
        xdt = xs * dt_e[:, sl]
        b16_ref[wslot, B_XDT, :, sl] = xdt.astype(BF16)
        b16_ref[wslot, B_XEND, :, sl] = (xdt * jnp.exp(alast_new[:, sl] - acs_new[:, sl])).astype(BF16)
        f32_ref[wslot, F_SKIP, :, sl] = xs * dsk_ref[:, sl]
        f32_ref[wslot, F_GATE, :, sl] = silu(z_ref[:, sl])

    pending = [prepare_bc] + [functools.partial(prepare_tile, t) for t in PR]

    def prepare_next():
        if pending:
            pending.pop(0)()

    if not pipelined:
        while pending:
            prepare_next()
            yield
    yield

    ti2 = lax.broadcasted_iota(jnp.int32, (C, 2 * C), 0)
    lane_tt = lax.broadcasted_iota(jnp.int32, (C, 2 * C), 1)
    sj2 = jnp.where(lane_tt >= C, lane_tt - C, lane_tt)
    causal2 = sj2 <= ti2
    eye2 = sj2 == ti2

    acs_e = f32_ref[rslot, F_ACS]
    s_decay = sdec_ref[rslot]
    bcm = bcs_ref[rslot]
    cb2, y_state = [], []
    for g in range(B_GROUPS):
        bm = bcm[:, g * D_STATE:(g + 1) * D_STATE]
        cm = bcm[:, (B_GROUPS + g) * D_STATE:(B_GROUPS + g + 1) * D_STATE]
        cb = lax.dot_general(cm, bm, NT, preferred_element_type=F32)
        cb2.append(jnp.concatenate([cb, cb], axis=1))
        gsl = slice(g * GW, (g + 1) * GW)
        s_old = s_ref[:, gsl]
        y_state.append(_dot(cm, s_old.astype(BF16)) * jnp.exp(acs_e[:, gsl]))
        s_ref[:, gsl] = s_old * s_decay[:, gsl] + lax.dot_general(
            bm, b16_ref[rslot, B_XEND, :, gsl], TN, preferred_element_type=F32)
        prepare_next()
        yield
    y_state = jnp.concatenate(y_state, axis=1)

    ys = []
    for p in PR:
        if C == HD:
            col_form = cols(acs_e, p)
        else:
            col_form = jnp.concatenate([acs_e[:, p * LANES:p * LANES + C],
                                        acs_e[:, p * LANES + HD:p * LANES + HD + C]], axis=1)
        row_form = jnp.sum(jnp.where(eye2, col_form, 0.0), axis=0, keepdims=True)
        m = jnp.where(causal2, jnp.exp(col_form - row_form), 0.0) * cb2[p // (B_PAIRS // B_GROUPS)]
        xdt_p = b16_ref[rslot, B_XDT, :, p * LANES:(p + 1) * LANES]
        ys.append(_dot(m.astype(BF16), _block_diag_rhs(xdt_p, lane_c, HD)))
        prepare_next()
        yield
    while pending:
        prepare_next()
    y = jnp.concatenate(ys, axis=1) + y_state

    y = (y + f32_ref[rslot, F_SKIP]) * f32_ref[rslot, F_GATE]
    outs = []
    for g in range(B_GROUPS):
        yg = y[:, g * GW:(g + 1) * GW]
        ms = jnp.mean(yg * yg, axis=-1, keepdims=True)
        outs.append(yg * lax.rsqrt(ms + EPS))
    yb_ref[...] = (jnp.concatenate(outs, axis=1) * nw_ref[...]).astype(BF16)
    yield MAIN_DONE

    @pl.when(c_idx == pl.num_programs(1) - 1)
    def _():
        for h in range(B_HEADS):
            os_ref[0, h] = s_ref[:, h * HD:(h + 1) * HD].T


N_PROJ_BLOCKS = 4
N_STATE = 3
N_SHARED_RWKV_WTS = 3
N_SHARED_MAMBA_WTS = 1
N_MIXER_OUT = 1 + N_STATE


def _mixer_kernel(*refs, chunk, pipelined, streams, stream_view, shared_state, n_rwkv_wts, n_mamba_wts,
                  n_rwkv_scratch):
    refs = list(refs)
    n_mixer_in = N_PROJ_BLOCKS + N_STATE
    n_rwkv_in = n_mixer_in + n_rwkv_wts
    n_in = n_rwkv_in + n_mixer_in + n_mamba_wts
    outs = refs[n_in:n_in + 2 * N_MIXER_OUT]
    scratch = refs[n_in + 2 * N_MIXER_OUT:]
    one = lambda ref, s: ref.at[pl.ds(s, 1)]

    def mixer_refs(ins, weights, outs, scratch, s):
        proj = [stream_view(r, s) for r in ins[:N_PROJ_BLOCKS]]
        state = [r if shared_state else one(r, s) for r in ins[N_PROJ_BLOCKS:n_mixer_in]]
        return (proj + state + weights + [stream_view(outs[0], s)]
                + [one(r, s) for r in outs[1:]] + [r.at[s] for r in scratch])

    rwkv_ins = refs[:n_rwkv_in]
    wa_up_ref, g_up_ref, ones_ref = rwkv_ins[n_mixer_in:n_mixer_in + N_SHARED_RWKV_WTS]
    rwkv_programs = [_rwkv_program(
        *mixer_refs(rwkv_ins, rwkv_ins[n_mixer_in + N_SHARED_RWKV_WTS:], outs[:N_MIXER_OUT],
                    scratch[:n_rwkv_scratch], s), chunk=chunk, pipelined=pipelined)
        for s in range(streams)]
    mamba_ins = refs[n_rwkv_in:n_in]
    (e3_ref,) = mamba_ins[n_mixer_in:n_mixer_in + N_SHARED_MAMBA_WTS]
    mamba_programs = [_mamba_program(
        *mixer_refs(mamba_ins, mamba_ins[n_mixer_in + N_SHARED_MAMBA_WTS:], outs[N_MIXER_OUT:],
                    scratch[n_rwkv_scratch:], s), chunk=chunk, pipelined=pipelined)
        for s in range(streams)]
    programs = [p for pair in zip(rwkv_programs, mamba_programs) for p in pair]

    def stacked_product(operands, weights_ref):
        product = _dot(jnp.concatenate(operands, axis=0), weights_ref[...])
        bounds = [0]
        for x in operands:
            bounds.append(bounds[-1] + x.shape[0])
        return [product[lo:hi] for lo, hi in zip(bounds[:-1], bounds[1:])]

    def serve(kind, operands):
        if kind == REQ_LOWRANK:
            l1, gl = zip(*operands)
            return list(zip(stacked_product(l1, wa_up_ref), stacked_product(gl, g_up_ref)))
        return stacked_product(operands, {REQ_EXPAND: e3_ref, REQ_HEAD_SUMS: ones_ref}[kind])

    to_send = [None] * len(programs)
    main_done = [False] * len(programs)
    while not all(main_done):
        requests = {}
        for i, prog in enumerate(programs):
            if main_done[i]:
                continue
            out, to_send[i] = prog.send(to_send[i]), None
            if out is MAIN_DONE:
                main_done[i] = True
            elif out is not None:
                requests.setdefault(out[0], []).append((i, out[1]))
        for kind, items in requests.items():
            for (i, _), reply in zip(items, serve(kind, [x for _, x in items])):
                to_send[i] = reply
    for prog in programs:
        for _ in prog:
            pass


def _mixers(proj, row_off, nb, nc, chunk, rwkv_state, mamba_state, shared_state, rwkv_wts, mamba_wts):
    C = chunk
    assert CONV_ROWS <= C <= B_HEAD_DIM
    W = A_WIDTH
    S = 2 if nb % 2 == 0 else 1
    pipelined, in_chunk, out_chunk = _chunk_pipeline(nc)
    slots = 2 if pipelined else 1
    if nc == 1:
        assert row_off % (S * C) == 0
        rb = row_off // (S * C)
        zspec = lambda width, col: pl.BlockSpec((S * C, width), lambda b, c: (rb + b, col // width))
        yspec = pl.BlockSpec((S * C, W), lambda b, c: (b, 0))
        y_shape = jax.ShapeDtypeStruct((nb * C, W), BF16)
        stream_view = lambda ref, s: ref.at[pl.ds(s * C, C)]
    else:
        assert row_off == 0 and proj.shape[0] == nb * nc * C
        proj = proj.reshape(nb, nc * C, PROJ_COLS)
        zspec = lambda width, col: pl.BlockSpec(
            (S, C, width), lambda b, c: (b, in_chunk(c), col // width))
        yspec = pl.BlockSpec((S, C, W), lambda b, c: (b, out_chunk(c), 0))
        y_shape = jax.ShapeDtypeStruct((nb, nc * C, W), BF16)
        stream_view = lambda ref, s: ref.at[s]
    full = lambda arr: pl.BlockSpec(arr.shape, lambda b, c: (0,) * arr.ndim)

    def state_specs(blocks):
        zeros = lambda blk: (0,) * (len(blk) - 1)
        if shared_state:
            ins = [pl.BlockSpec(blk, lambda b, c, z=zeros(blk): (0,) + z) for blk in blocks]
        else:
            ins = [pl.BlockSpec((S,) + blk[1:], lambda b, c, z=zeros(blk): (b,) + z) for blk in blocks]
        outs = [pl.BlockSpec((S,) + blk[1:], lambda b, c, z=zeros(blk): (b,) + z) for blk in blocks]
        shapes = [jax.ShapeDtypeStruct((nb,) + blk[1:], F32) for blk in blocks]
        return ins, outs, shapes

    r_in, r_out, r_shapes = state_specs(
        [(1, 1, 3 * W), (1, 1, LR_PAD), (1, A_HEADS, A_HEAD_DIM, A_HEAD_DIM)])
    m_in, m_out, m_shapes = state_specs(
        [(1, CONV_ROWS, B_WIDTH), (1, CONV_ROWS, BC_WIDTH), (1, B_HEADS, B_HEAD_DIM, D_STATE)])
    rwkv_in = [zspec(W, COL_R), zspec(W, COL_K), zspec(W, COL_V), zspec(LR_PAD, COL_LR)] + r_in \
        + [full(w) for w in rwkv_wts]
    mamba_in = [zspec(B_WIDTH, COL_Z), zspec(B_WIDTH, COL_X), zspec(BC_WIDTH, COL_BC),
                zspec(DT_PAD, COL_DT)] + m_in + [full(w) for w in mamba_wts]
    per_stream = lambda shape, dtype: pltpu.VMEM((S,) + shape, dtype)
    rwkv_scratch = [
        per_stream((1, 3 * W + LR_PAD), F32),
        per_stream((A_PAIRS, LANES, LANES), F32),
        per_stream((slots, 7, C, W), BF16),
        per_stream((slots, 2, C, W), F32),
        per_stream((slots, 1, W), F32),
    ]
    mamba_scratch = [
        per_stream((CONV_ROWS + C, B_WIDTH), F32),
        per_stream((CONV_ROWS + C, BC_WIDTH), F32),
        per_stream((D_STATE, B_WIDTH), F32),
        per_stream((slots, 3, C, B_WIDTH), F32),
        per_stream((slots, 2, C, B_WIDTH), BF16),
        per_stream((slots, C, BC_WIDTH), BF16),
        per_stream((slots, 1, B_WIDTH), F32),
    ]
    outs = pl.pallas_call(
        functools.partial(_mixer_kernel, chunk=C, pipelined=pipelined, streams=S,
                          stream_view=stream_view, shared_state=shared_state,
                          n_rwkv_wts=len(rwkv_wts), n_mamba_wts=len(mamba_wts),
                          n_rwkv_scratch=len(rwkv_scratch)),
        grid=(nb // S, nc + 1 if pipelined else nc),
        in_specs=rwkv_in + mamba_in,
        out_specs=[yspec] + r_out + [yspec] + m_out,
        out_shape=[y_shape] + r_shapes + [y_shape] + m_shapes,
        scratch_shapes=rwkv_scratch + mamba_scratch,
        compiler_params=_cparams(("parallel", "arbitrary")),
        name="mixers",
    )(*[proj] * N_PROJ_BLOCKS, *rwkv_state, *rwkv_wts, *[proj] * N_PROJ_BLOCKS, *mamba_state, *mamba_wts)
    outs = [o.reshape(nb * nc * C, W) if i % N_MIXER_OUT == 0 else o for i, o in enumerate(outs)]
    return outs[:N_MIXER_OUT], outs[N_MIXER_OUT:]


def _pad_cols(a, width):
    return jnp.pad(a, [(0, 0)] * (a.ndim - 1) + [(0, width - a.shape[-1])])


def kernel(x_prompt, x_sample, state_rwkv_shift, state_rwkv_wkv, state_ssm_conv, state_ssm, meta_tokens, norm_mix_w, w_in, rwkv_mu, rwkv_w0, rwkv_w_up, rwkv_a0, rwkv_a_up, rwkv_g_up, rwkv_k_k, rwkv_k_a, rwkv_r_k, rwkv_lnx_w, rwkv_lnx_b, ssm_conv_w, ssm_conv_b, ssm_dt_bias, ssm_A_log, ssm_D, ssm_norm_w, w_out, norm_ffn_w, ffn_w_gate, ffn_w_up, ffn_w_down, norm_final_w):
    depth = w_in.shape[0]
    assert depth == 1, "single-layer step"
    bp, lp, d = x_prompt.shape
    bs, ls, _ = x_sample.shape
    n_meta = meta_tokens.shape[0]
    assert d == D_MODEL and lp % CHUNK == 0 and ls <= CHUNK and n_meta <= CHUNK
    assert w_in.shape[-1] == A_COLS + 2 * B_WIDTH + BC_WIDTH + B_HEADS
    W = A_WIDTH

    wi = w_in[0]
    w_perm = jnp.concatenate([
        wi[:, 0:3 * W],
        wi[:, A_COLS:A_COLS + 2 * B_WIDTH + BC_WIDTH],
        _pad_cols(wi[:, 3 * W:A_COLS], LR_PAD),
        _pad_cols(wi[:, A_COLS + 2 * B_WIDTH + BC_WIDTH:], DT_COLS),
    ], axis=1).astype(BF16)
    nmw = norm_mix_w.reshape(1, D_MODEL)
    mu = rwkv_mu[0]
    mu_rkv = mu[None, 0:3 * W]
    mu_l = _pad_cols(mu[None, 3 * W:], LR_PAD)
    wa_up = jnp.zeros((LANES, 2 * W), F32)
    wa_up = wa_up.at[0:DECAY_RANK, 0:W].set(rwkv_w_up[0])
    wa_up = wa_up.at[DECAY_RANK:DECAY_RANK + AAA_RANK, W:].set(rwkv_a_up[0]).astype(BF16)
    g_up = jnp.pad(rwkv_g_up[0], ((0, 2 * LANES - GATE_RANK), (0, 0))).astype(BF16)
    lane_head = jnp.arange(LANES, dtype=jnp.int32) // A_HEAD_DIM
    ones_bd = (lane_head[:, None] == lane_head[None, :]).astype(BF16)
    rwkv_wts = (wa_up, g_up, ones_bd, mu_rkv, mu_l, rwkv_w0.reshape(1, W), rwkv_a0.reshape(1, W),
                rwkv_k_k.reshape(1, W), rwkv_k_a.reshape(1, W), rwkv_r_k.reshape(1, W),
                rwkv_lnx_w.reshape(1, W), rwkv_lnx_b.reshape(1, W))
    taps = ssm_conv_w[0].T
    cbias = ssm_conv_b.reshape(1, -1)
    src_head = jnp.where(jnp.arange(LANES) < 3 * B_HEADS, jnp.arange(LANES) % B_HEADS, -1)
    e3 = (src_head[:, None] == (jnp.arange(B_WIDTH) // B_HEAD_DIM)[None, :]).astype(BF16)
    mamba_wts = (e3, taps[:, :B_WIDTH], taps[:, B_WIDTH:], cbias[:, :B_WIDTH], cbias[:, B_WIDTH:],
                 _pad_cols(ssm_dt_bias.reshape(1, B_HEADS), DT_PAD),
                 _pad_cols(ssm_A_log.reshape(1, B_HEADS), DT_PAD),
                 jnp.repeat(ssm_D.reshape(1, B_HEADS), B_HEAD_DIM, axis=1),
                 ssm_norm_w.reshape(1, B_WIDTH))
    wo = w_out[0].astype(BF16)
    wo_a, wo_b = wo[:W], wo[W:]
    nfw = norm_ffn_w.reshape(1, D_MODEL)
    wg = ffn_w_gate[0].astype(BF16)
    wu = ffn_w_up[0].astype(BF16)
    wd = ffn_w_down[0].astype(BF16)
    nlw = norm_final_w.reshape(1, D_MODEL)

    tm_small = 384
    n_small = bs * ls + n_meta
    m_small = -(-n_small // tm_small) * tm_small
    x_small = jnp.concatenate([x_sample.reshape(bs * ls, d), meta_tokens,
                               jnp.zeros((m_small - n_small, d), F32)], axis=0)
    x_big = x_prompt.reshape(bp * lp, d)
    tm_big = math.gcd(bp * lp, 512)

    proj_small = _proj(x_small, nmw, w_perm, m_small)
    proj_big = _proj(x_big, nmw, w_perm, math.gcd(bp * lp, 1024))

    zeros = lambda *s: jnp.zeros(s, F32)
    sh = state_rwkv_shift[0]
    smp_rwkv = (sh[:, None, 0:3 * W], _pad_cols(sh[:, None, 3 * W:], LR_PAD), state_rwkv_wkv[0])
    conv0 = jnp.pad(state_ssm_conv[0], ((0, 0), (CONV_ROWS - (CONV_W - 1), 0), (0, 0)))
    smp_mamba = (conv0[..., :B_WIDTH], conv0[..., B_WIDTH:], state_ssm[0])
    meta_rwkv0 = (zeros(1, 1, 3 * W), zeros(1, 1, LR_PAD), zeros(1, A_HEADS, A_HEAD_DIM, A_HEAD_DIM))
    meta_mamba0 = (zeros(1, CONV_ROWS, B_WIDTH), zeros(1, CONV_ROWS, BC_WIDTH),
                   zeros(1, B_HEADS, B_HEAD_DIM, D_STATE))

    (ya_m, *meta_rwkv), (yb_m, *meta_mamba) = _mixers(
        proj_small, bs * ls, 1, 1, n_meta, meta_rwkv0, meta_mamba0, False, rwkv_wts, mamba_wts)
    (ya_s, *s_rwkv), (yb_s, *s_mamba) = _mixers(
        proj_small, 0, bs, 1, ls, smp_rwkv, smp_mamba, False, rwkv_wts, mamba_wts)
    (ya_p, *p_rwkv), (yb_p, *p_mamba) = _mixers(
        proj_big, 0, bp, lp // CHUNK, CHUNK, meta_rwkv, meta_mamba, True, rwkv_wts, mamba_wts)

    def tail(ya, yb, x, tm, tm_ffn1):
        x1, h2 = _outproj(x, ya, yb, wo_a, wo_b, nfw, tm)
        g = _ffn1(h2, wg, wu, tm_ffn1, 512)
        return _ffn2(g, wd, x1, nlw, tm, D_FF // 2)

    pad_rows = lambda *parts: jnp.concatenate(
        list(parts) + [jnp.zeros((m_small - n_small, parts[0].shape[1]), parts[0].dtype)], axis=0)
    y_small = tail(pad_rows(ya_s, ya_m), pad_rows(yb_s, yb_m), x_small, tm_small, m_small)
    y_big = tail(ya_p, yb_p, x_big, tm_big, math.gcd(bp * lp, 1024))

    def states_out(rw, mb):
        sh_rkv, sh_l, wkv = rw
        cx, cbc, ssm = mb
        shift = jnp.concatenate([sh_rkv[:, 0], sh_l[:, 0, :LOWRANK]], axis=-1)[None]
        conv = jnp.concatenate([cx, cbc], axis=-1)[None, :, CONV_ROWS - (CONV_W - 1):]
        return shift, wkv[None], conv, ssm[None]

    y_prompt = y_big.reshape(bp, lp, d)
    y_sample = y_small[:bs * ls].reshape(bs, ls, d)
    return (y_prompt, y_sample) + states_out(p_rwkv, p_mamba) + states_out(s_rwkv, s_mamba)
```

```python
import functools
import math

import jax
import jax.numpy as jnp
from jax import lax
from jax.experimental import pallas as pl
from jax.experimental.pallas import tpu as pltpu

F32 = jnp.float32
BF16 = jnp.bfloat16

D_MODEL = 2048
CHUNK = 64
EPS = 1e-6
LNX_EPS = 64e-5
A_WIDTH = 1024
A_HEADS = 16
A_HEAD_DIM = 64
A_PAIRS = A_HEADS // 2
DECAY_RANK = 64
AAA_RANK = 64
GATE_RANK = 160
B_WIDTH = 1024
B_HEADS = 16
B_HEAD_DIM = 64
B_PAIRS = B_HEADS // 2
B_GROUPS = 2
D_STATE = 128
CONV_W = 4
BC_WIDTH = 2 * B_GROUPS * D_STATE
A_COLS = 3 * A_WIDTH + DECAY_RANK + AAA_RANK + GATE_RANK
LOWRANK = DECAY_RANK + AAA_RANK + GATE_RANK
D_FF = 5632

LANES = 128
SUBLANES = 8
VMEM_LIMIT = 56 * 1024 * 1024

LR_PAD = 512
DT_PAD = 128
DT_COLS = 256
COL_R, COL_K, COL_V, COL_Z, COL_X = 0, 1024, 2048, 3072, 4096
COL_BC = 5120
COL_LR = 5632
COL_DT = 6144
PROJ_COLS = COL_DT + DT_COLS
PROJ_TN = 1280
CONV_ROWS = SUBLANES

NT = (((1,), (1,)), ((), ()))
TN = (((0,), (0,)), ((), ()))


def _cparams(sem):
    return pltpu.CompilerParams(dimension_semantics=sem, vmem_limit_bytes=VMEM_LIMIT)


def _dot(x, y):
    return jnp.dot(x, y, preferred_element_type=F32)


def _block_diag_rhs(x, lane, split):
    zero = jnp.zeros_like(x)
    return jnp.concatenate([jnp.where(lane < split, x, zero), jnp.where(lane < split, zero, x)], axis=0)


MAIN_DONE = "main done"


def _sigmoid(t):
    return 0.5 + 0.5 * jnp.tanh(0.5 * t)


def _chunk_pipeline(nc):
    if nc > 1:
        return True, (lambda c: jnp.minimum(c, nc - 1)), (lambda c: jnp.maximum(c - 1, 0))
    return False, (lambda c: c), (lambda c: c)


def _split3(x):
    hi = x.astype(BF16)
    r1 = x - hi.astype(F32)
    mid = r1.astype(BF16)
    return hi, mid, (r1 - mid.astype(F32)).astype(BF16)


def _proj_kernel(x_ref, nw_ref, w_ref, o_ref, h_ref):
    @pl.when(pl.program_id(1) == 0)
    def _():
        x = x_ref[...]
        ms = jnp.mean(x * x, axis=-1, keepdims=True)
        h_ref[...] = (x * lax.rsqrt(ms + EPS) * nw_ref[...]).astype(BF16)

    o_ref[...] = _dot(h_ref[...], w_ref[...])


def _proj(x, norm_w, w, tm):
    m = x.shape[0]
    return pl.pallas_call(
        _proj_kernel,
        grid=(m // tm, PROJ_COLS // PROJ_TN),
        in_specs=[
            pl.BlockSpec((tm, D_MODEL), lambda i, j: (i, 0)),
            pl.BlockSpec((1, D_MODEL), lambda i, j: (0, 0)),
            pl.BlockSpec((D_MODEL, PROJ_TN), lambda i, j: (0, j)),
        ],
        out_specs=pl.BlockSpec((tm, PROJ_TN), lambda i, j: (i, j)),
        out_shape=jax.ShapeDtypeStruct((m, PROJ_COLS), F32),
        scratch_shapes=[pltpu.VMEM((tm, D_MODEL), BF16)],
        compiler_params=_cparams(("parallel", "arbitrary")),
        name="proj",
    )(x, norm_w, w)


def _outproj_kernel(x_ref, ya_ref, yb_ref, wa_ref, wb_ref, nw_ref, x1_ref, h2_ref):
    x1 = x_ref[...] + _dot(ya_ref[...], wa_ref[...]) + _dot(yb_ref[...], wb_ref[...])
    x1_ref[...] = x1
    ms = jnp.mean(x1 * x1, axis=-1, keepdims=True)
    h2_ref[...] = (x1 * lax.rsqrt(ms + EPS) * nw_ref[...]).astype(BF16)


def _outproj(x, ya, yb, wa, wb, norm_w, tm):
    m = x.shape[0]
    row = lambda i: (i, 0)
    fixed = lambda i: (0, 0)
    return pl.pallas_call(
        _outproj_kernel,
        grid=(m // tm,),
        in_specs=[
            pl.BlockSpec((tm, D_MODEL), row),
            pl.BlockSpec((tm, A_WIDTH), row),
            pl.BlockSpec((tm, B_WIDTH), row),
            pl.BlockSpec((A_WIDTH, D_MODEL), fixed),
            pl.BlockSpec((B_WIDTH, D_MODEL), fixed),
            pl.BlockSpec((1, D_MODEL), fixed),
        ],
        out_specs=[pl.BlockSpec((tm, D_MODEL), row), pl.BlockSpec((tm, D_MODEL), row)],
        out_shape=[jax.ShapeDtypeStruct((m, D_MODEL), F32),
                   jax.ShapeDtypeStruct((m, D_MODEL), BF16)],
        compiler_params=_cparams(("parallel",)),
        name="outproj",
    )(x, ya, yb, wa, wb, norm_w)


def _ffn1_kernel(h_ref, wg_ref, wu_ref, o_ref):
    h = h_ref[...]
    g = _dot(h, wg_ref[...])
    u = _dot(h, wu_ref[...])
    o_ref[...] = (g * jax.nn.sigmoid(g) * u).astype(BF16)


def _ffn1(h2, wg, wu, tm, tn):
    m = h2.shape[0]
    return pl.pallas_call(
        _ffn1_kernel,
        grid=(m // tm, D_FF // tn),
        in_specs=[
            pl.BlockSpec((tm, D_MODEL), lambda i, j: (i, 0)),
            pl.BlockSpec((D_MODEL, tn), lambda i, j: (0, j)),
            pl.BlockSpec((D_MODEL, tn), lambda i, j: (0, j)),
        ],
        out_specs=pl.BlockSpec((tm, tn), lambda i, j: (i, j)),
        out_shape=jax.ShapeDtypeStruct((m, D_FF), BF16),
        compiler_params=_cparams(("parallel", "parallel")),
        name="ffn1",
    )(h2, wg, wu)


def _ffn2_kernel(g_ref, wd_ref, x1_ref, nw_ref, o_ref):
    k = pl.program_id(1)

    @pl.when(k == 0)
    def _():
        o_ref[...] = x1_ref[...] + _dot(g_ref[...], wd_ref[...])

    @pl.when(jnp.logical_and(k > 0, k < pl.num_programs(1) - 1))
    def _():
        o_ref[...] += _dot(g_ref[...], wd_ref[...])

    @pl.when(k == pl.num_programs(1) - 1)
    def _():
        x2 = o_ref[...] + _dot(g_ref[...], wd_ref[...])
        ms = jnp.mean(x2 * x2, axis=-1, keepdims=True)
        o_ref[...] = x2 * lax.rsqrt(ms + EPS) * nw_ref[...]


def _ffn2(g, wd, x1, norm_w, tm, tk):
    m = g.shape[0]
    assert D_FF // tk >= 2
    return pl.pallas_call(
        _ffn2_kernel,
        grid=(m // tm, D_FF // tk),
        in_specs=[
            pl.BlockSpec((tm, tk), lambda i, k: (i, k)),
            pl.BlockSpec((tk, D_MODEL), lambda i, k: (k, 0)),
            pl.BlockSpec((tm, D_MODEL), lambda i, k: (i, 0)),
            pl.BlockSpec((1, D_MODEL), lambda i, k: (0, 0)),
        ],
        out_specs=pl.BlockSpec((tm, D_MODEL), lambda i, k: (i, 0)),
        out_shape=jax.ShapeDtypeStruct((m, D_MODEL), F32),
        compiler_params=_cparams(("parallel", "arbitrary")),
        name="ffn2",
    )(g, wd, x1, norm_w)


def _shift_rows(z, prev_row):
    rolled = pltpu.roll(z, 1, axis=0)
    row = lax.broadcasted_iota(jnp.int32, z.shape, 0)
    return jnp.where(row == 0, prev_row, rolled)


def _rwkv_program(zr_ref, zk_ref, zv_ref, zl_ref, sh_rkv_ref, sh_l_ref, s0_ref,
                 mu_rkv_ref, mu_l_ref, w0_ref, a0_ref,
                 kk_ref, ka_ref, rk_ref, lnw_ref, lnb_ref, ones_ref,
                 ya_ref, osh_rkv_ref, osh_l_ref, os_ref,
                 prev_ref, s_ref, ops_ref, aux_ref, plast_ref, *, chunk, pipelined):
    C = chunk
    W = A_WIDTH
    HD = A_HEAD_DIM
    c_idx = pl.program_id(1)
    wslot = c_idx % 2 if pipelined else 0
    rslot = 1 - wslot if pipelined else 0
    PR = range(A_PAIRS)
    cols = lambda t, p: t[:, p * LANES:(p + 1) * LANES]
    OP_AT, OP_RT, OP_KH, OP_BH, OP_KE, OP_BE, OP_V = range(7)
    AUX_BONUS, AUX_GATE = range(2)

    @pl.when(c_idx == 0)
    def _():
        prev_ref[:, 0:3 * W] = sh_rkv_ref[0]
        prev_ref[:, 3 * W:3 * W + LR_PAD] = sh_l_ref[0]
        zero = jnp.zeros((HD, HD), F32)
        for p in PR:
            s_ref[p] = jnp.concatenate(
                [jnp.concatenate([s0_ref[0, 2 * p], zero], axis=1),
                 jnp.concatenate([zero, s0_ref[0, 2 * p + 1]], axis=1)], axis=0)
        if pipelined:
            ops_ref[1] = jnp.zeros(ops_ref.shape[1:], BF16)
            aux_ref[1] = jnp.zeros(aux_ref.shape[1:], F32)
            plast_ref[1] = jnp.ones(plast_ref.shape[1:], F32)

    yield
    ti2 = lax.broadcasted_iota(jnp.int32, (C, 2 * C), 0)
    lane_tt = lax.broadcasted_iota(jnp.int32, (C, 2 * C), 1)
    sj2 = jnp.where(lane_tt >= C, lane_tt - C, lane_tt)
    incl2 = sj2 <= ti2
    strict2 = sj2 < ti2
    tri2_b = jnp.where(incl2, 1.0, 0.0).astype(BF16)
    lane_c = lax.broadcasted_iota(jnp.int32, (C, LANES), 1)
    bd_i = lax.broadcasted_iota(jnp.int32, (LANES, LANES), 0) // HD
    bd_j = lax.broadcasted_iota(jnp.int32, (LANES, LANES), 1) // HD
    bd_mask = bd_i == bd_j
    n_sq = int(math.log2(C)) - 1
    bd_ch = lambda x: _block_diag_rhs(x, lane_c, HD)
    bd_tt = lambda x: _block_diag_rhs(x, lane_tt, C)

    def mixed(z, prev_lo, mu):
        width = z.shape[1]
        zp = _shift_rows(z, prev_ref[:, prev_lo:prev_lo + width])
        prev_ref[:, prev_lo:prev_lo + width] = z[C - 1:C, :]
        return z + mu * (zp - z)

    zl_raw = zl_ref[...]
    zl = mixed(zl_raw, 3 * W, mu_l_ref[...])
    osh_l_ref[0] = zl_raw[C - 1:C, :]
    l1 = zl[:, 0:LANES]
    l1 = jnp.where(lane_c < DECAY_RANK, jnp.tanh(l1), l1)
    gl = _sigmoid(zl[:, LANES:3 * LANES])
    wa, gate = yield l1.astype(BF16), gl.astype(BF16)
    aux_ref[wslot, AUX_GATE] = gate

    def hsum_tiles(tiles):
        s = _dot(jnp.concatenate(tiles, axis=0).astype(BF16), ones_ref[...])
        return [s[i * C:(i + 1) * C] for i in range(len(tiles))]

    def prepare_tile(p):
        lo, hi = p * LANES, (p + 1) * LANES
        sl = slice(lo, hi)

        def mixed_rkv(z_ref, base):
            z = z_ref[:, sl]
            osh_rkv_ref[0, :, base + lo:base + hi] = z[C - 1:C, :]
            return mixed(z, base + lo, mu_rkv_ref[:, base + lo:base + hi])

        hsum = lambda x: hsum_tiles([x])[0]
        r = mixed_rkv(zr_ref, 0)
        k = mixed_rkv(zk_ref, W)
        v = mixed_rkv(zv_ref, 2 * W)
        wlog = -jax.nn.softplus(-(w0_ref[:, sl] + wa[:, sl])) - 0.5
        lw = -jnp.exp(wlog)
        a = _sigmoid(a0_ref[:, sl] + wa[:, W + lo:W + hi])
        kk = k * kk_ref[:, sl]
        k2 = k * (1.0 + (a - 1.0) * ka_ref[:, sl])
        kk = kk * lax.rsqrt(jnp.maximum(hsum(kk * kk), 1e-24))
        ka = kk * a
        lw_hi = lw.astype(BF16)
        lw_lo = (lw - lw_hi.astype(F32)).astype(BF16)
        cum = _dot(tri2_b, jnp.concatenate([lw_hi, lw_lo], axis=0))
        cum_last = cum[C - 1:C, :]
        e_neg = jnp.exp(-cum)
        e_end = jnp.exp(cum_last - cum)
        ops_ref[wslot, OP_AT, :, sl] = (kk * jnp.exp(cum - lw)).astype(BF16)
        ops_ref[wslot, OP_RT, :, sl] = (r * jnp.exp(cum)).astype(BF16)
        ops_ref[wslot, OP_KH, :, sl] = (k2 * e_neg).astype(BF16)
        ops_ref[wslot, OP_BH, :, sl] = (ka * e_neg).astype(BF16)
        ops_ref[wslot, OP_KE, :, sl] = (k2 * e_end).astype(BF16)
        ops_ref[wslot, OP_BE, :, sl] = (-ka * e_end).astype(BF16)
        ops_ref[wslot, OP_V, :, sl] = v.astype(BF16)
        plast_ref[wslot, :, sl] = jnp.exp(cum_last)
        aux_ref[wslot, AUX_BONUS, :, sl] = hsum(r * k2 * rk_ref[:, sl]) * v

    pending = list(PR)

    def prepare_next():
        if pending:
            prepare_tile(pending.pop(0))

    if not pipelined:
        while pending:
            prepare_next()

    op = lambda i, p: ops_ref[rslot, i, :, p * LANES:(p + 1) * LANES]
    s_old = [s_ref[p] for p in PR]
    v_b = [op(OP_V, p) for p in PR]
    lhs = [jnp.concatenate([op(OP_AT, p), op(OP_RT, p)], axis=0) for p in PR]
    rhs = [jnp.concatenate([bd_ch(op(OP_KH, p)), bd_ch(op(OP_BH, p))], axis=0) for p in PR]
    upd_rhs = [jnp.concatenate([op(OP_KE, p), op(OP_BE, p)], axis=0) for p in PR]
    p_last = plast_ref[rslot]
    m1 = [lax.dot_general(lhs[p], rhs[p], NT, preferred_element_type=F32) for p in PR]
    m2 = [lax.dot_general(lhs[p], s_old[p].astype(BF16), NT, preferred_element_type=F32) for p in PR]
    prepare_next()
    yield
    a_k = [jnp.where(strict2, m1[p][0:C, 0:2 * C], 0.0).astype(BF16) for p in PR]
    a_b = [jnp.where(strict2, m1[p][0:C, 2 * C:4 * C], 0.0) for p in PR]
    b_kb = [jnp.concatenate([jnp.where(incl2, m1[p][C:2 * C, 0:2 * C], 0.0),
                             jnp.where(incl2, -m1[p][C:2 * C, 2 * C:4 * C], 0.0)], axis=1).astype(BF16)
            for p in PR]
    bd_v = [bd_ch(v_b[p]) for p in PR]
    u = [m2[p][0:C] + _dot(a_k[p], bd_v[p]) for p in PR]
    prepare_next()
    yield
    pw_b = [a_b[p].astype(BF16) for p in PR]
    u = [u[p] - _dot(pw_b[p], bd_ch(u[p].astype(BF16))) for p in PR]
    prepare_next()
    yield
    for _ in range(n_sq):
        pw_b = [_dot(pw_b[p], bd_tt(pw_b[p])).astype(BF16) for p in PR]
        u = [u[p] + _dot(pw_b[p], bd_ch(u[p].astype(BF16))) for p in PR]
        prepare_next()
        yield
    u_b = [u[p].astype(BF16) for p in PR]
    ys = [m2[p][C:2 * C] + _dot(b_kb[p], jnp.concatenate([bd_v[p], bd_ch(u_b[p])], axis=0)) for p in PR]
    upd = [lax.dot_general(jnp.concatenate([v_b[p], u_b[p]], axis=0), upd_rhs[p],
                           TN, preferred_element_type=F32) for p in PR]
    for p in PR:
        s_ref[p] = s_old[p] * cols(p_last, p) + jnp.where(bd_mask, upd[p], 0.0)
    while pending:
        prepare_next()
    yield

    inv_n = 1.0 / A_HEAD_DIM
    mean = hsum_tiles(ys)
    d = [ys[p] - mean[p] * inv_n for p in PR]
    var = hsum_tiles([d[p] * d[p] for p in PR])
    yn = jnp.concatenate([d[p] * lax.rsqrt(var[p] * inv_n + LNX_EPS) for p in PR], axis=1)
    y = yn * lnw_ref[...] + lnb_ref[...] + aux_ref[rslot, AUX_BONUS]
    ya_ref[...] = (y * aux_ref[rslot, AUX_GATE]).astype(BF16)
    yield MAIN_DONE

    @pl.when(c_idx == pl.num_programs(1) - 1)
    def _():
        for p in PR:
            s_pair = s_ref[p]
            os_ref[0, 2 * p] = s_pair[0:HD, 0:HD]
            os_ref[0, 2 * p + 1] = s_pair[HD:2 * HD, HD:2 * HD]


def _mamba_program(z_ref, x_ref, bc_ref, dt_ref, cx0_ref, cbc0_ref, s0_ref,
                  cwx_ref, cwbc_ref, cbx_ref, cbbc_ref, dtb_ref, alog_ref, dsk_ref, nw_ref,
                  yb_ref, ocx_ref, ocbc_ref, os_ref,
                  cx_ref, cbc_ref, s_ref, f32_ref, b16_ref, bcs_ref, sdec_ref, *, chunk, pipelined):
    C = chunk
    c_idx = pl.program_id(1)
    wslot = c_idx % 2 if pipelined else 0
    rslot = 1 - wslot if pipelined else 0
    GW = B_WIDTH // B_GROUPS
    HD = B_HEAD_DIM
    PR = range(B_PAIRS)
    cols = lambda t, p: t[:, p * LANES:(p + 1) * LANES]
    F_SKIP, F_GATE, F_ACS = range(3)
    B_XDT, B_XEND = range(2)

    @pl.when(c_idx == 0)
    def _():
        cx_ref[0:CONV_ROWS, :] = cx0_ref[0]
        cbc_ref[0:CONV_ROWS, :] = cbc0_ref[0]
        for p in PR:
            s_ref[:, p * LANES:(p + 1) * LANES] = jnp.concatenate(
                [s0_ref[0, 2 * p].T, s0_ref[0, 2 * p + 1].T], axis=1)
        if pipelined:
            f32_ref[1] = jnp.zeros(f32_ref.shape[1:], F32)
            b16_ref[1] = jnp.zeros(b16_ref.shape[1:], BF16)
            bcs_ref[1] = jnp.zeros(bcs_ref.shape[1:], BF16)
            sdec_ref[1] = jnp.ones(sdec_ref.shape[1:], F32)

    yield
    silu = lambda t: t * _sigmoid(t)

    def causal_conv(raw, ext_ref, taps_ref, bias_ref, sl):
        ext_ref[CONV_ROWS:CONV_ROWS + C, sl] = raw
        acc = bias_ref[:, sl] + taps_ref[CONV_W - 1:CONV_W, sl] * raw
        for j in range(CONV_W - 1):
            start = CONV_ROWS - (CONV_W - 1 - j)
            acc = acc + taps_ref[j:j + 1, sl] * ext_ref[start:start + C, sl]
        ext_ref[0:CONV_ROWS, sl] = raw[C - CONV_ROWS:C, :]
        return silu(acc)

    lane_c = lax.broadcasted_iota(jnp.int32, (C, LANES), 1)
    valid = lane_c < B_HEADS
    dt = jnp.where(valid, jax.nn.softplus(dt_ref[...] + dtb_ref[...]), 0.0)
    d_a = dt * (-jnp.exp(alog_ref[...]))

    ti3 = lax.broadcasted_iota(jnp.int32, (C, 3 * C), 0)
    sj3 = lax.broadcasted_iota(jnp.int32, (C, 3 * C), 1)
    sj3 = jnp.where(sj3 >= 2 * C, sj3 - 2 * C, jnp.where(sj3 >= C, sj3 - C, sj3))
    tri3 = jnp.where(sj3 <= ti3, 1.0, 0.0).astype(BF16)
    a_cs = _dot(tri3, jnp.concatenate(_split3(d_a), axis=0))

    def pack3(t):
        hi, mid, lo = _split3(t)
        return (hi.astype(F32) + pltpu.roll(mid.astype(F32), B_HEADS, axis=1)
                + pltpu.roll(lo.astype(F32), 2 * B_HEADS, axis=1)).astype(BF16)

    expanded = yield jnp.concatenate([pack3(dt), pack3(a_cs)], axis=0)
    dt_e = expanded[0:C]
    acs_new = expanded[C:2 * C]
    alast_new = acs_new[C - 1:C, :]
    f32_ref[wslot, F_ACS] = acs_new
    sdec_ref[wslot] = jnp.exp(alast_new)

    def prepare_bc():
        bcs_ref[wslot] = causal_conv(bc_ref[...], cbc_ref, cwbc_ref, cbbc_ref,
                                     slice(0, BC_WIDTH)).astype(BF16)
        ocbc_ref[0] = cbc_ref[0:CONV_ROWS, :]

    def prepare_tile(t):
        sl = slice(t * LANES, (t + 1) * LANES)
        xs = causal_conv(x_ref[:, sl], cx_ref, cwx_ref, cbx_ref, sl)
        ocx_ref[0, :, sl] = cx_ref[0:CONV_ROWS, sl]
        xdt = xs * dt_e[:, sl]
        b16_ref[wslot, B_XDT, :, sl] = xdt.astype(BF16)
        b16_ref[wslot, B_XEND, :, sl] = (xdt * jnp.exp(alast_new[:, sl] - acs_new[:, sl])).astype(BF16)
        f32_ref[wslot, F_SKIP, :, sl] = xs * dsk_ref[:, sl]
        f32_ref[wslot, F_GATE, :, sl] = silu(z_ref[:, sl])

    pending = [prepare_bc] + [functools.partial(prepare_tile, t) for t in PR]

    def prepare_next():
        if pending:
            pending.pop(0)()

    if not pipelined:
        while pending:
            prepare_next()
            yield
    yield

    ti2 = lax.broadcasted_iota(jnp.int32, (C, 2 * C), 0)
    lane_tt = lax.broadcasted_iota(jnp.int32, (C, 2 * C), 1)
    sj2 = jnp.where(lane_tt >= C, lane_tt - C, lane_tt)
    causal2 = sj2 <= ti2
    eye2 = sj2 == ti2

    acs_e = f32_ref[rslot, F_ACS]
    s_decay = sdec_ref[rslot]
    bcm = bcs_ref[rslot]
    cb2, y_state = [], []
    for g in range(B_GROUPS):
        bm = bcm[:, g * D_STATE:(g + 1) * D_STATE]
        cm = bcm[:, (B_GROUPS + g) * D_STATE:(B_GROUPS + g + 1) * D_STATE]
        cb = lax.dot_general(cm, bm, NT, preferred_element_type=F32)
        cb2.append(jnp.concatenate([cb, cb], axis=1))
        gsl = slice(g * GW, (g + 1) * GW)
        s_old = s_ref[:, gsl]
        y_state.append(_dot(cm, s_old.astype(BF16)) * jnp.exp(acs_e[:, gsl]))
        s_ref[:, gsl] = s_old * s_decay[:, gsl] + lax.dot_general(
            bm, b16_ref[rslot, B_XEND, :, gsl], TN, preferred_element_type=F32)
        prepare_next()
        yield
    y_state = jnp.concatenate(y_state, axis=1)

    ys = []
    for p in PR:
        if C == HD:
            col_form = cols(acs_e, p)
        else:
            col_form = jnp.concatenate([acs_e[:, p * LANES:p * LANES + C],
                                        acs_e[:, p * LANES + HD:p * LANES + HD + C]], axis=1)
        row_form = jnp.sum(jnp.where(eye2, col_form, 0.0), axis=0, keepdims=True)
        m = jnp.where(causal2, jnp.exp(col_form - row_form), 0.0) * cb2[p // (B_PAIRS // B_GROUPS)]
        xdt_p = b16_ref[rslot, B_XDT, :, p * LANES:(p + 1) * LANES]
        ys.append(_dot(m.astype(BF16), _block_diag_rhs(xdt_p, lane_c, HD)))
        prepare_next()
        yield
    while pending:
        prepare_next()
    y = jnp.concatenate(ys, axis=1) + y_state

    y = (y + f32_ref[rslot, F_SKIP]) * f32_ref[rslot, F_GATE]
    outs = []
    for g in range(B_GROUPS):
        yg = y[:, g * GW:(g + 1) * GW]
        ms = jnp.mean(yg * yg, axis=-1, keepdims=True)
        outs.append(yg * lax.rsqrt(ms + EPS))
    yb_ref[...] = (jnp.concatenate(outs, axis=1) * nw_ref[...]).astype(BF16)
    yield MAIN_DONE

    @pl.when(c_idx == pl.num_programs(1) - 1)
    def _():
        for h in range(B_HEADS):
            os_ref[0, h] = s_ref[:, h * HD:(h + 1) * HD].T


N_PROJ_BLOCKS = 4
N_STATE = 3
N_SHARED_RWKV_WTS = 2
N_SHARED_MAMBA_WTS = 1
N_MIXER_OUT = 1 + N_STATE


def _mixer_kernel(*refs, chunk, pipelined, streams, stream_view, shared_state, n_rwkv_wts, n_mamba_wts,
                  n_rwkv_scratch):
    refs = list(refs)
    n_mixer_in = N_PROJ_BLOCKS + N_STATE
    n_rwkv_in = n_mixer_in + n_rwkv_wts
    n_in = n_rwkv_in + n_mixer_in + n_mamba_wts
    outs = refs[n_in:n_in + 2 * N_MIXER_OUT]
    scratch = refs[n_in + 2 * N_MIXER_OUT:]
    one = lambda ref, s: ref.at[pl.ds(s, 1)]

    def mixer_refs(ins, weights, outs, scratch, s):
        proj = [stream_view(r, s) for r in ins[:N_PROJ_BLOCKS]]
        state = [r if shared_state else one(r, s) for r in ins[N_PROJ_BLOCKS:n_mixer_in]]
        return (proj + state + weights + [stream_view(outs[0], s)]
                + [one(r, s) for r in outs[1:]] + [r.at[s] for r in scratch])

    rwkv_ins = refs[:n_rwkv_in]
    wa_up_ref, g_up_ref = rwkv_ins[n_mixer_in:n_mixer_in + N_SHARED_RWKV_WTS]
    rwkv_programs = [_rwkv_program(
        *mixer_refs(rwkv_ins, rwkv_ins[n_mixer_in + N_SHARED_RWKV_WTS:], outs[:N_MIXER_OUT],
                    scratch[:n_rwkv_scratch], s), chunk=chunk, pipelined=pipelined)
        for s in range(streams)]
    mamba_ins = refs[n_rwkv_in:n_in]
    (e3_ref,) = mamba_ins[n_mixer_in:n_mixer_in + N_SHARED_MAMBA_WTS]
    mamba_programs = [_mamba_program(
        *mixer_refs(mamba_ins, mamba_ins[n_mixer_in + N_SHARED_MAMBA_WTS:], outs[N_MIXER_OUT:],
                    scratch[n_rwkv_scratch:], s), chunk=chunk, pipelined=pipelined)
        for s in range(streams)]
    programs = rwkv_programs + mamba_programs
    for prog in programs:
        next(prog)
    l1, gl = zip(*[next(prog) for prog in rwkv_programs])
    wa = _dot(jnp.concatenate(l1, axis=0), wa_up_ref[...])
    gate = _dot(jnp.concatenate(gl, axis=0), g_up_ref[...])
    packed = [next(prog) for prog in mamba_programs]
    expanded = _dot(jnp.concatenate(packed, axis=0), e3_ref[...])
    for s in range(streams):
        rows = slice(s * chunk, (s + 1) * chunk)
        rwkv_programs[s].send((wa[rows], gate[rows]))
        mamba_programs[s].send(expanded[2 * s * chunk:2 * (s + 1) * chunk])
    main_done = [False] * len(programs)
    while not all(main_done):
        for i, prog in enumerate(programs):
            if not main_done[i]:
                main_done[i] = next(prog) is MAIN_DONE
    for prog in programs:
        for _ in prog:
            pass


def _mixers(proj, row_off, nb, nc, chunk, rwkv_state, mamba_state, shared_state, rwkv_wts, mamba_wts):
    C = chunk
    assert CONV_ROWS <= C <= B_HEAD_DIM
    W = A_WIDTH
    S = 2 if nb % 2 == 0 else 1
    pipelined, in_chunk, out_chunk = _chunk_pipeline(nc)
    slots = 2 if pipelined else 1
    if nc == 1:
        assert row_off % (S * C) == 0
        rb = row_off // (S * C)
        zspec = lambda width, col: pl.BlockSpec((S * C, width), lambda b, c: (rb + b, col // width))
        yspec = pl.BlockSpec((S * C, W), lambda b, c: (b, 0))
        y_shape = jax.ShapeDtypeStruct((nb * C, W), BF16)
        stream_view = lambda ref, s: ref.at[pl.ds(s * C, C)]
    else:
        assert row_off == 0 and proj.shape[0] == nb * nc * C
        proj = proj.reshape(nb, nc * C, PROJ_COLS)
        zspec = lambda width, col: pl.BlockSpec(
            (S, C, width), lambda b, c: (b, in_chunk(c), col // width))
        yspec = pl.BlockSpec((S, C, W), lambda b, c: (b, out_chunk(c), 0))
        y_shape = jax.ShapeDtypeStruct((nb, nc * C, W), BF16)
        stream_view = lambda ref, s: ref.at[s]
    full = lambda arr: pl.BlockSpec(arr.shape, lambda b, c: (0,) * arr.ndim)

    def state_specs(blocks):
        zeros = lambda blk: (0,) * (len(blk) - 1)
        if shared_state:
            ins = [pl.BlockSpec(blk, lambda b, c, z=zeros(blk): (0,) + z) for blk in blocks]
        else:
            ins = [pl.BlockSpec((S,) + blk[1:], lambda b, c, z=zeros(blk): (b,) + z) for blk in blocks]
        outs = [pl.BlockSpec((S,) + blk[1:], lambda b, c, z=zeros(blk): (b,) + z) for blk in blocks]
        shapes = [jax.ShapeDtypeStruct((nb,) + blk[1:], F32) for blk in blocks]
        return ins, outs, shapes

    r_in, r_out, r_shapes = state_specs(
        [(1, 1, 3 * W), (1, 1, LR_PAD), (1, A_HEADS, A_HEAD_DIM, A_HEAD_DIM)])
    m_in, m_out, m_shapes = state_specs(
        [(1, CONV_ROWS, B_WIDTH), (1, CONV_ROWS, BC_WIDTH), (1, B_HEADS, B_HEAD_DIM, D_STATE)])
    rwkv_in = [zspec(W, COL_R), zspec(W, COL_K), zspec(W, COL_V), zspec(LR_PAD, COL_LR)] + r_in \
        + [full(w) for w in rwkv_wts]
    mamba_in = [zspec(B_WIDTH, COL_Z), zspec(B_WIDTH, COL_X), zspec(BC_WIDTH, COL_BC),
                zspec(DT_PAD, COL_DT)] + m_in + [full(w) for w in mamba_wts]
    per_stream = lambda shape, dtype: pltpu.VMEM((S,) + shape, dtype)
    rwkv_scratch = [
        per_stream((1, 3 * W + LR_PAD), F32),
        per_stream((A_PAIRS, LANES, LANES), F32),
        per_stream((slots, 7, C, W), BF16),
        per_stream((slots, 2, C, W), F32),
        per_stream((slots, 1, W), F32),
    ]
    mamba_scratch = [
        per_stream((CONV_ROWS + C, B_WIDTH), F32),
        per_stream((CONV_ROWS + C, BC_WIDTH), F32),
        per_stream((D_STATE, B_WIDTH), F32),
        per_stream((slots, 3, C, B_WIDTH), F32),
        per_stream((slots, 2, C, B_WIDTH), BF16),
        per_stream((slots, C, BC_WIDTH), BF16),
        per_stream((slots, 1, B_WIDTH), F32),
    ]
    outs = pl.pallas_call(
        functools.partial(_mixer_kernel, chunk=C, pipelined=pipelined, streams=S,
                          stream_view=stream_view, shared_state=shared_state,
                          n_rwkv_wts=len(rwkv_wts), n_mamba_wts=len(mamba_wts),
                          n_rwkv_scratch=len(rwkv_scratch)),
        grid=(nb // S, nc + 1 if pipelined else nc),
        in_specs=rwkv_in + mamba_in,
        out_specs=[yspec] + r_out + [yspec] + m_out,
        out_shape=[y_shape] + r_shapes + [y_shape] + m_shapes,
        scratch_shapes=rwkv_scratch + mamba_scratch,
        compiler_params=_cparams(("parallel", "arbitrary")),
        name="mixers",
    )(*[proj] * N_PROJ_BLOCKS, *rwkv_state, *rwkv_wts, *[proj] * N_PROJ_BLOCKS, *mamba_state, *mamba_wts)
    outs = [o.reshape(nb * nc * C, W) if i % N_MIXER_OUT == 0 else o for i, o in enumerate(outs)]
    return outs[:N_MIXER_OUT], outs[N_MIXER_OUT:]


def _pad_cols(a, width):
    return jnp.pad(a, [(0, 0)] * (a.ndim - 1) + [(0, width - a.shape[-1])])


def kernel(x_prompt, x_sample, state_rwkv_shift, state_rwkv_wkv, state_ssm_conv, state_ssm, meta_tokens, norm_mix_w, w_in, rwkv_mu, rwkv_w0, rwkv_w_up, rwkv_a0, rwkv_a_up, rwkv_g_up, rwkv_k_k, rwkv_k_a, rwkv_r_k, rwkv_lnx_w, rwkv_lnx_b, ssm_conv_w, ssm_conv_b, ssm_dt_bias, ssm_A_log, ssm_D, ssm_norm_w, w_out, norm_ffn_w, ffn_w_gate, ffn_w_up, ffn_w_down, norm_final_w):
    depth = w_in.shape[0]
    assert depth == 1, "single-layer step"
    bp, lp, d = x_prompt.shape
    bs, ls, _ = x_sample.shape
    n_meta = meta_tokens.shape[0]
    assert d == D_MODEL and lp % CHUNK == 0 and ls <= CHUNK and n_meta <= CHUNK
    assert w_in.shape[-1] == A_COLS + 2 * B_WIDTH + BC_WIDTH + B_HEADS
    W = A_WIDTH

    wi = w_in[0]
    w_perm = jnp.concatenate([
        wi[:, 0:3 * W],
        wi[:, A_COLS:A_COLS + 2 * B_WIDTH + BC_WIDTH],
        _pad_cols(wi[:, 3 * W:A_COLS], LR_PAD),
        _pad_cols(wi[:, A_COLS + 2 * B_WIDTH + BC_WIDTH:], DT_COLS),
    ], axis=1).astype(BF16)
    nmw = norm_mix_w.reshape(1, D_MODEL)
    mu = rwkv_mu[0]
    mu_rkv = mu[None, 0:3 * W]
    mu_l = _pad_cols(mu[None, 3 * W:], LR_PAD)
    wa_up = jnp.zeros((LANES, 2 * W), F32)
    wa_up = wa_up.at[0:DECAY_RANK, 0:W].set(rwkv_w_up[0])
    wa_up = wa_up.at[DECAY_RANK:DECAY_RANK + AAA_RANK, W:].set(rwkv_a_up[0]).astype(BF16)
    g_up = jnp.pad(rwkv_g_up[0], ((0, 2 * LANES - GATE_RANK), (0, 0))).astype(BF16)
    lane_head = jnp.arange(LANES, dtype=jnp.int32) // A_HEAD_DIM
    ones_bd = (lane_head[:, None] == lane_head[None, :]).astype(BF16)
    rwkv_wts = (wa_up, g_up, mu_rkv, mu_l, rwkv_w0.reshape(1, W), rwkv_a0.reshape(1, W),
                rwkv_k_k.reshape(1, W), rwkv_k_a.reshape(1, W), rwkv_r_k.reshape(1, W),
                rwkv_lnx_w.reshape(1, W), rwkv_lnx_b.reshape(1, W), ones_bd)
    taps = ssm_conv_w[0].T
    cbias = ssm_conv_b.reshape(1, -1)
    src_head = jnp.where(jnp.arange(LANES) < 3 * B_HEADS, jnp.arange(LANES) % B_HEADS, -1)
    e3 = (src_head[:, None] == (jnp.arange(B_WIDTH) // B_HEAD_DIM)[None, :]).astype(BF16)
    mamba_wts = (e3, taps[:, :B_WIDTH], taps[:, B_WIDTH:], cbias[:, :B_WIDTH], cbias[:, B_WIDTH:],
                 _pad_cols(ssm_dt_bias.reshape(1, B_HEADS), DT_PAD),
                 _pad_cols(ssm_A_log.reshape(1, B_HEADS), DT_PAD),
                 jnp.repeat(ssm_D.reshape(1, B_HEADS), B_HEAD_DIM, axis=1),
                 ssm_norm_w.reshape(1, B_WIDTH))
    wo = w_out[0].astype(BF16)
    wo_a, wo_b = wo[:W], wo[W:]
    nfw = norm_ffn_w.reshape(1, D_MODEL)
    wg = ffn_w_gate[0].astype(BF16)
    wu = ffn_w_up[0].astype(BF16)
    wd = ffn_w_down[0].astype(BF16)
    nlw = norm_final_w.reshape(1, D_MODEL)

    tm_small = 384
    n_small = bs * ls + n_meta
    m_small = -(-n_small // tm_small) * tm_small
    x_small = jnp.concatenate([x_sample.reshape(bs * ls, d), meta_tokens,
                               jnp.zeros((m_small - n_small, d), F32)], axis=0)
    x_big = x_prompt.reshape(bp * lp, d)
    tm_big = math.gcd(bp * lp, 512)

    proj_small = _proj(x_small, nmw, w_perm, m_small)
    proj_big = _proj(x_big, nmw, w_perm, math.gcd(bp * lp, 1024))

    zeros = lambda *s: jnp.zeros(s, F32)
    sh = state_rwkv_shift[0]
    smp_rwkv = (sh[:, None, 0:3 * W], _pad_cols(sh[:, None, 3 * W:], LR_PAD), state_rwkv_wkv[0])
    conv0 = jnp.pad(state_ssm_conv[0], ((0, 0), (CONV_ROWS - (CONV_W - 1), 0), (0, 0)))
    smp_mamba = (conv0[..., :B_WIDTH], conv0[..., B_WIDTH:], state_ssm[0])
    meta_rwkv0 = (zeros(1, 1, 3 * W), zeros(1, 1, LR_PAD), zeros(1, A_HEADS, A_HEAD_DIM, A_HEAD_DIM))
    meta_mamba0 = (zeros(1, CONV_ROWS, B_WIDTH), zeros(1, CONV_ROWS, BC_WIDTH),
                   zeros(1, B_HEADS, B_HEAD_DIM, D_STATE))

    (ya_m, *meta_rwkv), (yb_m, *meta_mamba) = _mixers(
        proj_small, bs * ls, 1, 1, n_meta, meta_rwkv0, meta_mamba0, False, rwkv_wts, mamba_wts)
    (ya_s, *s_rwkv), (yb_s, *s_mamba) = _mixers(
        proj_small, 0, bs, 1, ls, smp_rwkv, smp_mamba, False, rwkv_wts, mamba_wts)
    (ya_p, *p_rwkv), (yb_p, *p_mamba) = _mixers(
        proj_big, 0, bp, lp // CHUNK, CHUNK, meta_rwkv, meta_mamba, True, rwkv_wts, mamba_wts)

    def tail(ya, yb, x, tm, tm_ffn1):
        x1, h2 = _outproj(x, ya, yb, wo_a, wo_b, nfw, tm)
        g = _ffn1(h2, wg, wu, tm_ffn1, 512)
        return _ffn2(g, wd, x1, nlw, tm, D_FF // 2)

    pad_rows = lambda *parts: jnp.concatenate(
        list(parts) + [jnp.zeros((m_small - n_small, parts[0].shape[1]), parts[0].dtype)], axis=0)
    y_small = tail(pad_rows(ya_s, ya_m), pad_rows(yb_s, yb_m), x_small, tm_small, m_small)
    y_big = tail(ya_p, yb_p, x_big, tm_big, math.gcd(bp * lp, 1024))

    def states_out(rw, mb):
        sh_rkv, sh_l, wkv = rw
        cx, cbc, ssm = mb
        shift = jnp.concatenate([sh_rkv[:, 0], sh_l[:, 0, :LOWRANK]], axis=-1)[None]
        conv = jnp.concatenate([cx, cbc], axis=-1)[None, :, CONV_ROWS - (CONV_W - 1):]
        return shift, wkv[None], conv, ssm[None]

    y_prompt = y_big.reshape(bp, lp, d)
    y_sample = y_small[:bs * ls].reshape(bs, ls, d)
    return (y_prompt, y_sample) + states_out(p_rwkv, p_mamba) + states_out(s_rwkv, s_mamba)
```
